```python
import math
import jax, jax.numpy as jnp
from jax import lax
import numpy as np

D_MODEL = 1024
BATCH = 8
SEQ = 2048
DEPTH = 2
DEC_BATCH = 16
DEC_SEQ = 32
PAST_LEN = 1024

CHUNK = 64
N_META = 16
Q_BLOCK = 128
N_GROUPS = 4
GROUP_W = D_MODEL // N_GROUPS
HEAD_DIM = 64
N_HEADS = GROUP_W // HEAD_DIM
LRU_BLOCKS = 4
LRU_BW = GROUP_W // LRU_BLOCKS
LRU_C = 8.0
CONV_W = 4
D_FF = 2816
FFN_CONV = 3
ALPHA = (2.0 * DEPTH) ** 0.25
BETA_INIT = (8.0 * DEPTH) ** -0.25
LN_EPS = 1e-5
RMS_EPS = 1e-6
IN_SPLITS = (GROUP_W, GROUP_W,
             GROUP_W, GROUP_W, GROUP_W, N_HEADS,
             GROUP_W, GROUP_W, GROUP_W,
             GROUP_W, GROUP_W, GROUP_W, N_HEADS, N_HEADS, GROUP_W)
D_IN = 12 * GROUP_W + 3 * N_HEADS
STATE_NAMES = ('lru_conv', 'lru_h', 'fox_k', 'fox_v', 'fox_logf', 'sb_k', 'sb_v', 'dn_conv', 'dn_S', 'ffn_conv')

kernel_name = 'hybrid_streaming_encoder_step'


def layer_norm(x, g, b):
    xf = x.astype(jnp.float32)
    mu = jnp.mean(xf, -1, keepdims=True)
    var = jnp.mean(jnp.square(xf - mu), -1, keepdims=True)
    return ((xf - mu) * lax.rsqrt(var + LN_EPS) * g + b).astype(x.dtype)


def rms_norm(x, g):
    xf = x.astype(jnp.float32)
    return xf * lax.rsqrt(jnp.mean(xf * xf, -1, keepdims=True) + RMS_EPS) * g


def l2_normalize(x):
    xf = x.astype(jnp.float32)
    return xf * lax.rsqrt(jnp.sum(xf * xf, -1, keepdims=True) + RMS_EPS)


def split_projection(proj):
    idx, acc = [], 0
    for s in IN_SPLITS[:-1]:
        acc += s
        idx.append(acc)
    return jnp.split(proj, idx, axis=-1)


def causal_dwconv(u, buf, w, b=None):
    width, L = w.shape[0], u.shape[1]
    full = jnp.concatenate([buf.astype(u.dtype), u], axis=1)
    out = full[:, 0:L] * w[0]
    for i in range(1, width):
        out = out + full[:, i:i + L] * w[i]
    if b is not None:
        out = out + b
    return out, full[:, L:]


def rg_lru(x, h0, w_a, b_a, w_x, b_x, lam):
    B, L, W = x.shape
    xf = x.astype(jnp.float32)
    xb = xf.reshape(B, L, LRU_BLOCKS, LRU_BW)
    r = jax.nn.sigmoid(jnp.einsum('blni,nij->blnj', xb, w_a).reshape(B, L, W) + b_a)
    i = jax.nn.sigmoid(jnp.einsum('blni,nij->blnj', xb, w_x).reshape(B, L, W) + b_x)
    log_a = -LRU_C * r * jax.nn.softplus(-lam)
    a = jnp.exp(log_a)
    b = jnp.sqrt(-jnp.expm1(2.0 * log_a)) * (i * xf)
    b = b.at[:, 0].add(a[:, 0] * h0.astype(jnp.float32))

    def combine(lhs, rhs):
        a1, b1 = lhs
        a2, b2 = rhs
        return a1 * a2, a2 * b1 + b2

    _, h = lax.associative_scan(combine, (a, b), axis=1)
    return h, h[:, -1]


def fox_attend(q, k, v, f_q, f_k, q_pos, k_pos):
    s = jnp.einsum('bqhd,bkhd->bhqk', q, k).astype(jnp.float32) * (HEAD_DIM ** -0.5)
    s = s + (jnp.moveaxis(f_q, 2, 1)[..., :, None] - jnp.moveaxis(f_k, 2, 1)[..., None, :])
    s = jnp.where(k_pos[None, :] <= q_pos[:, None], s, -jnp.inf)
    p = jax.nn.softmax(s, axis=-1)
    return jnp.einsum('bhqk,bkhd->bqhd', p, v.astype(jnp.float32))


def sb_attend(q, k, v, q_pos, k_pos):
    z = jnp.einsum('bqhd,bkhd->bhqk', q, k).astype(jnp.float32) * (HEAD_DIM ** -0.5)
    mask = k_pos[None, :] < q_pos[:, None]
    log_keep = jnp.where(mask, jax.nn.log_sigmoid(-z), 0.0)
    later = lax.cumsum(log_keep, axis=3, reverse=True) - log_keep
    w = jnp.where(mask, jnp.exp(jax.nn.log_sigmoid(z) + later), 0.0)
    return jnp.einsum('bhqk,bkhd->bqhd', w, v.astype(jnp.float32))


def sweep_query_blocks(fn, q_args, q_pos):
    Lq = q_pos.shape[0]
    if Lq <= Q_BLOCK:
        return fn(*q_args, q_pos)
    nb = -(-Lq // Q_BLOCK)
    pad = nb * Q_BLOCK - Lq

    def to_blocks(a):
        a = jnp.pad(a, [(0, 0), (0, pad)] + [(0, 0)] * (a.ndim - 2))
        return jnp.moveaxis(a.reshape(a.shape[0], nb, Q_BLOCK, *a.shape[2:]), 1, 0)

    qp = jnp.pad(q_pos, (0, pad), mode='edge').reshape(nb, Q_BLOCK)
    out = lax.map(lambda blk: fn(*blk[0], blk[1]), (tuple(to_blocks(a) for a in q_args), qp))
    out = jnp.moveaxis(out, 0, 1)
    return out.reshape(out.shape[0], nb * Q_BLOCK, *out.shape[3:])[:, :Lq]


def gated_delta_rule(q, k, v, g, beta, S0, front):
    B, L, H, DK = q.shape
    DV = v.shape[-1]
    n = -(-(front + L) // CHUNK)
    back = n * CHUNK - front - L

    def blocks(a):
        a = jnp.pad(a, [(0, 0), (front, back)] + [(0, 0)] * (a.ndim - 2))
        a = a.reshape(B, n, CHUNK, H, *a.shape[3:])
        return jnp.moveaxis(a, 3, 1)

    q, k, v, g, beta = (blocks(a) for a in (q, k, v, g, beta))
    gc = jnp.cumsum(g, axis=-1)
    idx = jnp.arange(CHUNK)
    causal = idx[:, None] >= idx[None, :]
    strict = idx[:, None] > idx[None, :]
    decay = jnp.exp(jnp.where(causal, gc[..., :, None] - gc[..., None, :], -jnp.inf))
    kb = k * beta[..., None]
    m = jnp.where(strict, jnp.einsum('bhntd,bhnsd->bhnts', kb, k) * decay, 0.0)
    eye = jnp.eye(CHUNK, dtype=m.dtype)
    rhs = jnp.concatenate([v * beta[..., None], kb * jnp.exp(gc)[..., None]], axis=-1)
    sol = lax.linalg.triangular_solve(m + eye, rhs, left_side=True, lower=True, unit_diagonal=True)
    u, w = sol[..., :DV], sol[..., DV:]
    a_in = jnp.einsum('bhntd,bhnsd->bhnts', q, k) * decay

    def step(S, blk):
        qi, ki, ui, wi, gi, ai = blk
        v_new = ui - jnp.einsum('bhtk,bhkv->bhtv', wi, S)
        o = (jnp.einsum('bhtk,bhkv->bhtv', qi * jnp.exp(gi)[..., None], S)
             + jnp.einsum('bhts,bhsv->bhtv', ai, v_new))
        g_last = gi[..., -1]
        S = (S * jnp.exp(g_last)[..., None, None]
             + jnp.einsum('bhtk,bhtv->bhkv', ki * jnp.exp(g_last[..., None] - gi)[..., None], v_new))
        return S, o

    xs = tuple(jnp.moveaxis(a, 2, 0) for a in (q, k, u, w, gc, a_in))
    S, o = lax.scan(step, S0, xs)
    o = jnp.moveaxis(o, 0, 2).reshape(B, H, n * CHUNK, DV)
    return jnp.moveaxis(o, 1, 2)[:, front:front + L], S


def token_mixers(h, l, P, st, front):
    B, L, _ = h.shape
    f32 = jnp.float32
    heads = lambda a: a.reshape(B, L, N_HEADS, HEAD_DIM)
    proj = h @ P['w_in'][l] + P['b_in'][l]
    (lru_x, lru_gate, fox_q, fox_k, fox_v, fox_f, sb_q, sb_k, sb_v,
     dn_q, dn_k, dn_v, dn_a, dn_b, dn_gate) = split_projection(proj)
    past = st['fox_k'].shape[1]
    q_pos = past + jnp.arange(L)
    k_pos = jnp.arange(past + L)

    xa, lru_conv_new = causal_dwconv(lru_x, st['lru_conv'], P['lru_conv_w'][l], P['lru_conv_b'][l])
    ha, h_last = rg_lru(xa, st['lru_h'], P['lru_w_a'][l], P['lru_b_a'][l],
                        P['lru_w_x'][l], P['lru_b_x'][l], P['lru_lambda'][l])
    out_a = jax.nn.gelu(lru_gate.astype(f32)) * ha

    fk_new, fv_new = heads(fox_k), heads(fox_v)
    logf_new = jax.nn.log_sigmoid(fox_f.astype(f32))
    fk_all = jnp.concatenate([st['fox_k'].astype(fk_new.dtype), fk_new], axis=1)
    fv_all = jnp.concatenate([st['fox_v'].astype(fv_new.dtype), fv_new], axis=1)
    F = jnp.cumsum(jnp.concatenate([st['fox_logf'].astype(f32), logf_new], axis=1), axis=1)
    out_b = sweep_query_blocks(
        lambda q, fq, qp: fox_attend(q, fk_all, fv_all, fq, F, qp, k_pos),
        (heads(fox_q), F[:, past:]), q_pos)

    sk_new, sv_new = heads(sb_k), heads(sb_v)
    sk_all = jnp.concatenate([st['sb_k'].astype(sk_new.dtype), sk_new], axis=1)
    sv_all = jnp.concatenate([st['sb_v'].astype(sv_new.dtype), sv_new], axis=1)
    out_c = sweep_query_blocks(
        lambda q, qp: sb_attend(q, sk_all, sv_all, qp, k_pos), (heads(sb_q),), q_pos)

    qkv, dn_conv_new = causal_dwconv(jnp.concatenate([dn_q, dn_k, dn_v], axis=-1),
                                     st['dn_conv'], P['dn_conv_w'][l])
    q_d, k_d, v_d = jnp.split(jax.nn.silu(qkv.astype(f32)), 3, axis=-1)
    q_d = l2_normalize(heads(q_d)) * (HEAD_DIM ** -0.5)
    k_d = l2_normalize(heads(k_d))
    g = -jnp.exp(P['dn_a_log'][l]) * jax.nn.softplus(dn_a.astype(f32) + P['dn_dt_bias'][l])
    beta = jax.nn.sigmoid(dn_b.astype(f32))
    o_d, S_new = gated_delta_rule(q_d, k_d, heads(v_d), g, beta, st['dn_S'].astype(f32), front)
    o_d = rms_norm(o_d, P['dn_norm_g'][l]) * jax.nn.silu(heads(dn_gate).astype(f32))

    gains = P['grp_norm_g'][l]
    cat = jnp.concatenate([rms_norm(out_a, gains[0]),
                           rms_norm(out_b.reshape(B, L, GROUP_W), gains[1]),
                           rms_norm(out_c.reshape(B, L, GROUP_W), gains[2]),
                           o_d.reshape(B, L, GROUP_W)], axis=-1).astype(h.dtype)
    out = cat @ P['w_o'][l]
    new = {'lru_conv': lru_conv_new, 'lru_h': h_last, 'fox_k': fk_new, 'fox_v': fv_new,
           'fox_logf': logf_new, 'sb_k': sk_new, 'sb_v': sv_new,
           'dn_conv': dn_conv_new, 'dn_S': S_new}
    return out, new


def conv_ffn(x, buf, w_in, conv_w, conv_b, w_out):
    gate, up = jnp.split(x @ w_in, 2, axis=-1)
    gate_c, new_buf = causal_dwconv(gate, buf, conv_w, conv_b)
    return (jax.nn.gelu(gate_c) * up) @ w_out, new_buf


def trunk(x, st, P, front):
    x = layer_norm(x, P['ln_in_g'], P['ln_in_b'])
    new = {n: [] for n in STATE_NAMES}
    for l in range(DEPTH):
        st_l = {n: st[n][l] for n in STATE_NAMES}
        m, ns = token_mixers(x, l, P, st_l, front)
        x = layer_norm(ALPHA * x + m, P['ln1_g'][l], P['ln1_b'][l])
        f, fbuf = conv_ffn(x, st_l['ffn_conv'], P['ffn_w_in'][l], P['ffn_conv_w'][l],
                           P['ffn_conv_b'][l], P['ffn_w_out'][l])
        x = layer_norm(ALPHA * x + f, P['ln2_g'][l], P['ln2_b'][l])
        ns['ffn_conv'] = fbuf
        for n in STATE_NAMES:
            new[n].append(ns[n])
    return x, {n: jnp.stack(new[n]) for n in STATE_NAMES}


def empty_state(batch, dtype):
    f32 = jnp.float32
    return {'lru_conv': jnp.zeros((DEPTH, batch, CONV_W - 1, GROUP_W), dtype),
            'lru_h': jnp.zeros((DEPTH, batch, GROUP_W), f32),
            'fox_k': jnp.zeros((DEPTH, batch, 0, N_HEADS, HEAD_DIM), dtype),
            'fox_v': jnp.zeros((DEPTH, batch, 0, N_HEADS, HEAD_DIM), dtype),
            'fox_logf': jnp.zeros((DEPTH, batch, 0, N_HEADS), f32),
            'sb_k': jnp.zeros((DEPTH, batch, 0, N_HEADS, HEAD_DIM), dtype),
            'sb_v': jnp.zeros((DEPTH, batch, 0, N_HEADS, HEAD_DIM), dtype),
            'dn_conv': jnp.zeros((DEPTH, batch, CONV_W - 1, 3 * GROUP_W), dtype),
            'dn_S': jnp.zeros((DEPTH, batch, N_HEADS, HEAD_DIM, HEAD_DIM), f32),
            'ffn_conv': jnp.zeros((DEPTH, batch, FFN_CONV - 1, D_FF), dtype)}


def setup_inputs(seed: int = 0) -> dict:
    key = jax.random.key(seed)
    ks = iter(jax.random.split(key, 48))
    nrm = lambda shape, scale=1.0: jax.random.normal(next(ks), shape, jnp.float32) * scale
    gain = lambda shape: 1.0 + nrm(shape, 0.02)
    u_lam = jax.random.uniform(next(ks), (DEPTH, GROUP_W), minval=0.9, maxval=0.999)
    s_lam = u_lam ** (1.0 / LRU_C)
    dt = jnp.exp(jax.random.uniform(next(ks), (DEPTH, N_HEADS), minval=math.log(1e-3), maxval=math.log(1e-1)))
    a_init = jax.random.uniform(next(ks), (DEPTH, N_HEADS), minval=1.0, maxval=16.0)
    return {
        'x_prompt': nrm((BATCH, SEQ, D_MODEL)),
        'x_sample': nrm((DEC_BATCH, DEC_SEQ, D_MODEL)),
        'state_lru_conv': nrm((DEPTH, DEC_BATCH, CONV_W - 1, GROUP_W)),
        'state_lru_h': nrm((DEPTH, DEC_BATCH, GROUP_W), 0.5),
        'cache_fox_k': nrm((DEPTH, DEC_BATCH, PAST_LEN, N_HEADS, HEAD_DIM)),
        'cache_fox_v': nrm((DEPTH, DEC_BATCH, PAST_LEN, N_HEADS, HEAD_DIM)),
        'cache_fox_logf': jax.nn.log_sigmoid(nrm((DEPTH, DEC_BATCH, PAST_LEN, N_HEADS)) + 2.0),
        'cache_sb_k': nrm((DEPTH, DEC_BATCH, PAST_LEN, N_HEADS, HEAD_DIM)),
        'cache_sb_v': nrm((DEPTH, DEC_BATCH, PAST_LEN, N_HEADS, HEAD_DIM)),
        'state_dn_conv': nrm((DEPTH, DEC_BATCH, CONV_W - 1, 3 * GROUP_W)),
        'state_dn_S': nrm((DEPTH, DEC_BATCH, N_HEADS, HEAD_DIM, HEAD_DIM), 0.1),
        'state_ffn_conv': nrm((DEPTH, DEC_BATCH, FFN_CONV - 1, D_FF)),
        'meta_tokens': nrm((N_META, D_MODEL)),
        'ln_in_g': gain((D_MODEL,)),
        'ln_in_b': nrm((D_MODEL,), 0.02),
        'w_in': nrm((DEPTH, D_MODEL, D_IN), D_MODEL ** -0.5),
        'b_in': nrm((DEPTH, D_IN), 0.02),
        'lru_conv_w': nrm((DEPTH, CONV_W, GROUP_W), CONV_W ** -0.5),
        'lru_conv_b': nrm((DEPTH, GROUP_W), 0.02),
        'lru_w_a': nrm((DEPTH, LRU_BLOCKS, LRU_BW, LRU_BW), LRU_BW ** -0.5),
        'lru_b_a': nrm((DEPTH, GROUP_W), 0.02),
        'lru_w_x': nrm((DEPTH, LRU_BLOCKS, LRU_BW, LRU_BW), LRU_BW ** -0.5),
        'lru_b_x': nrm((DEPTH, GROUP_W), 0.02),
        'lru_lambda': jnp.log(s_lam) - jnp.log1p(-s_lam),
        'dn_conv_w': nrm((DEPTH, CONV_W, 3 * GROUP_W), CONV_W ** -0.5),
        'dn_a_log': jnp.log(a_init),
        'dn_dt_bias': dt + jnp.log(-jnp.expm1(-dt)),
        'dn_norm_g': gain((DEPTH, HEAD_DIM)),
        'grp_norm_g': gain((DEPTH, 3, GROUP_W)),
        'w_o': nrm((DEPTH, D_MODEL, D_MODEL), BETA_INIT * D_MODEL ** -0.5),
        'ln1_g': gain((DEPTH, D_MODEL)),
        'ln1_b': nrm((DEPTH, D_MODEL), 0.02),
        'ffn_w_in': nrm((DEPTH, D_MODEL, 2 * D_FF), D_MODEL ** -0.5),
        'ffn_conv_w': nrm((DEPTH, FFN_CONV, D_FF), FFN_CONV ** -0.5),
        'ffn_conv_b': nrm((DEPTH, D_FF), 0.02),
        'ffn_w_out': nrm((DEPTH, D_FF, D_MODEL), BETA_INIT * D_FF ** -0.5),
        'ln2_g': gain((DEPTH, D_MODEL)),
        'ln2_b': nrm((DEPTH, D_MODEL), 0.02),
    }


def reference(x_prompt, x_sample, state_lru_conv, state_lru_h, cache_fox_k, cache_fox_v,
              cache_fox_logf, cache_sb_k, cache_sb_v, state_dn_conv, state_dn_S, state_ffn_conv,
              meta_tokens, ln_in_g, ln_in_b, w_in, b_in, lru_conv_w, lru_conv_b, lru_w_a, lru_b_a,
              lru_w_x, lru_b_x, lru_lambda, dn_conv_w, dn_a_log, dn_dt_bias, dn_norm_g, grp_norm_g,
              w_o, ln1_g, ln1_b, ffn_w_in, ffn_conv_w, ffn_conv_b, ffn_w_out, ln2_g, ln2_b):
    P = {'ln_in_g': ln_in_g, 'ln_in_b': ln_in_b, 'w_in': w_in, 'b_in': b_in,
         'lru_conv_w': lru_conv_w, 'lru_conv_b': lru_conv_b, 'lru_w_a': lru_w_a, 'lru_b_a': lru_b_a,
         'lru_w_x': lru_w_x, 'lru_b_x': lru_b_x, 'lru_lambda': lru_lambda,
         'dn_conv_w': dn_conv_w, 'dn_a_log': dn_a_log, 'dn_dt_bias': dn_dt_bias, 'dn_norm_g': dn_norm_g,
         'grp_norm_g': grp_norm_g, 'w_o': w_o, 'ln1_g': ln1_g, 'ln1_b': ln1_b,
         'ffn_w_in': ffn_w_in, 'ffn_conv_w': ffn_conv_w, 'ffn_conv_b': ffn_conv_b, 'ffn_w_out': ffn_w_out,
         'ln2_g': ln2_g, 'ln2_b': ln2_b}
    bp = x_prompt.shape[0]
    meta = jnp.broadcast_to(meta_tokens.astype(x_prompt.dtype), (bp, N_META, D_MODEL))
    xp = jnp.concatenate([meta, x_prompt], axis=1)
    yp, ps = trunk(xp, empty_state(bp, x_prompt.dtype), P, (-N_META) % CHUNK)
    y_prompt = yp[:, N_META:]
    st_in = {'lru_conv': state_lru_conv, 'lru_h': state_lru_h, 'fox_k': cache_fox_k, 'fox_v': cache_fox_v,
             'fox_logf': cache_fox_logf, 'sb_k': cache_sb_k, 'sb_v': cache_sb_v,
             'dn_conv': state_dn_conv, 'dn_S': state_dn_S, 'ffn_conv': state_ffn_conv}
    y_sample, ss = trunk(x_sample, st_in, P, 0)
    return (y_prompt, y_sample,
            ps['lru_conv'], ps['lru_h'], ps['fox_k'], ps['fox_v'], ps['fox_logf'],
            ps['sb_k'], ps['sb_v'], ps['dn_conv'], ps['dn_S'], ps['ffn_conv'],
            ss['lru_conv'], ss['lru_h'], ss['fox_k'], ss['fox_v'], ss['fox_logf'],
            ss['sb_k'], ss['sb_v'], ss['dn_conv'], ss['dn_S'], ss['ffn_conv'])
```

```python
import functools

import jax
import jax.numpy as jnp
from jax import lax
from jax.experimental import pallas as pl
from jax.experimental.pallas import tpu as pltpu

f32 = jnp.float32
bf16 = jnp.bfloat16

D_MODEL = 1024
GROUP_W = 256
N_HEADS = 4
HEAD_DIM = 64
N_META = 16
LRU_C = 8.0
LN_EPS = 1e-5
RMS_EPS = 1e-6
ATT_TILE = 128
DN_CHUNK = 64
NEG = -1e30
SMALL_W = 128
VMEM_LIMIT_BYTES = 56 * 1024 * 1024

NN = (((1,), (0,)), ((), ()))
NT = (((1,), (1,)), ((), ()))
TN = (((0,), (0,)), ((), ()))


def _dot(a, b, dims=NN):
    return lax.dot_general(a, b, dims, preferred_element_type=f32)


def _split2(x):
    hi = x.astype(bf16)
    lo = (x - hi.astype(f32)).astype(bf16)
    return hi, lo


def _split3(x):
    p0 = x.astype(bf16)
    r = x - p0.astype(f32)
    p1 = r.astype(bf16)
    p2 = (r - p1.astype(f32)).astype(bf16)
    return p0, p1, p2


def _dot_sel_rhs(x, sel, dims=NN):
    p0, p1, p2 = _split3(x)
    return _dot(p0, sel, dims) + _dot(p1, sel, dims) + _dot(p2, sel, dims)


def _dot_sel_lhs(sel, x, dims=NN):
    p0, p1, p2 = _split3(x)
    return _dot(sel, p0, dims) + _dot(sel, p1, dims) + _dot(sel, p2, dims)


def _mm3(a, b):
    ah, al = _split2(a)
    bh, bl = _split2(b)
    return _dot(ah, bh) + _dot(ah, bl) + _dot(al, bh)


def _iota(shape, dim):
    return lax.broadcasted_iota(jnp.int32, shape, dim)


def _head_masks(width=GROUP_W):
    lane_head = _iota((1, width), 1) >> 6
    return [lane_head == h for h in range(N_HEADS)]


def _softplus(x):
    return jnp.maximum(x, 0.0) + jnp.log1p(jnp.exp(-jnp.abs(x)))


def _log_sigmoid(x):
    return jnp.minimum(x, 0.0) - jnp.log1p(jnp.exp(-jnp.abs(x)))


def _layer_norm(x, g, b):
    mu = jnp.mean(x, -1, keepdims=True)
    xc = x - mu
    var = jnp.mean(xc * xc, -1, keepdims=True)
    return xc * lax.rsqrt(var + LN_EPS) * g + b


def _rms_norm(x, g):
    return x * lax.rsqrt(jnp.mean(x * x, -1, keepdims=True) + RMS_EPS) * g


def _row_tile(n, cap, mult=16):
    best = None
    for d in range(mult, min(n, cap) + 1, mult):
        if n % d == 0:
            best = d
    return best if best is not None else n


def _const_spec(shape):
    nd = len(shape)
    return pl.BlockSpec(shape, lambda *_: (0,) * nd, pipeline_mode=pl.Buffered(1))


def _params(*sem):
    return pltpu.CompilerParams(dimension_semantics=sem, vmem_limit_bytes=VMEM_LIMIT_BYTES)


PROJ_WIDTHS = (512, 256, 256, 256, 256, 256, 256, 768, 256, SMALL_W)


def _proj_kernel(do_ln, x_ref, g_ref, b_ref, w_ref, bias_ref, *outs):
    x = x_ref[...]
    if do_ln:
        x = _layer_norm(x, g_ref[...], b_ref[...])
        outs[0][...] = x
        outs = outs[1:]
    xb = x.astype(bf16)
    col = 0
    for o, wd in zip(outs, PROJ_WIDTHS):
        o[...] = _dot(xb, w_ref[:, col:col + wd]) + bias_ref[:, col:col + wd]
        col += wd


def _proj(x2d, ln_g, ln_b, w, bias, do_ln):
    n = x2d.shape[0]
    tm = _row_tile(n, 768)
    wtot = w.shape[1]
    row = lambda wd: pl.BlockSpec((tm, wd), lambda i: (i, 0))
    out_shape = [jax.ShapeDtypeStruct((n, wd), f32) for wd in PROJ_WIDTHS]
    out_specs = [row(wd) for wd in PROJ_WIDTHS]
    if do_ln:
        out_shape = [jax.ShapeDtypeStruct((n, D_MODEL), f32)] + out_shape
        out_specs = [row(D_MODEL)] + out_specs
    return pl.pallas_call(
        functools.partial(_proj_kernel, do_ln),
        grid=(n // tm,),
        in_specs=[row(D_MODEL), _const_spec((1, D_MODEL)), _const_spec((1, D_MODEL)),
                  _const_spec((D_MODEL, wtot)), _const_spec((1, wtot))],
        out_specs=out_specs,
        out_shape=out_shape,
        compiler_params=_params("parallel"),
        name="proj",
    )(x2d, ln_g, ln_b, w, bias)


def _lru_kernel(L, R1, lru_ref, cst_ref, h0_ref, cw_ref, cb_ref, wa_ref, wx_ref, ba_ref, bx_ref,
                lam_ref, gain_ref, ya_ref, cnew_ref, hlast_ref, xpad_s, a_s, b_s):
    W = GROUP_W
    xpad_s[pl.ds(0, 8), :] = jnp.zeros((8, W), f32)
    xpad_s[pl.ds(5, 3), :] = cst_ref[0]
    xpad_s[pl.ds(8, L), :] = lru_ref[0, :, 0:W]
    cnew_ref[0] = xpad_s[pl.ds(L + 5, 3), :]

    sp = _softplus(-lam_ref[...])
    cw = cw_ref[...]
    for t in range(L // R1):
        r0 = t * R1
        xa = cb_ref[...]
        for i in range(4):
            xa = xa + cw[i:i + 1, :] * xpad_s[pl.ds(5 + r0 + i, R1), :]
        xb = xa.astype(bf16)
        r = jax.nn.sigmoid(_dot(xb, wa_ref[...]) + ba_ref[...])
        ig = jax.nn.sigmoid(_dot(xb, wx_ref[...]) + bx_ref[...])
        log_a = (-LRU_C) * r * sp
        y2 = 2.0 * log_a
        one_m_a2 = jnp.tanh(-0.5 * y2) * (1.0 + jnp.exp(y2))
        a_s[pl.ds(r0, R1), :] = jnp.exp(log_a)
        b_s[pl.ds(r0, R1), :] = jnp.sqrt(one_m_a2) * (ig * xa)

    rows8 = _iota((8, W), 0)

    def scan_body(g, hprev):
        off = pl.multiple_of(g * 8, 8)
        A = a_s[pl.ds(off, 8), :]
        Bv = b_s[pl.ds(off, 8), :]
        for s in (1, 2, 4):
            keep = rows8 >= s
            a_sh = jnp.where(keep, pltpu.roll(A, s, 0), 1.0)
            b_sh = jnp.where(keep, pltpu.roll(Bv, s, 0), 0.0)
            Bv = A * b_sh + Bv
            A = A * a_sh
        h = A * hprev + Bv
        b_s[pl.ds(off, 8), :] = h
        return h[7:8, :]

    hlast_ref[0] = lax.fori_loop(0, L // 8, scan_body, h0_ref[0])

    for t in range(L // R1):
        r0 = t * R1
        o = jax.nn.gelu(lru_ref[0, pl.ds(r0, R1), W:2 * W]) * b_s[pl.ds(r0, R1), :]
        ya_ref[0, pl.ds(r0, R1), :] = _rms_norm(o, gain_ref[...]).astype(bf16)


def _lru(lru3d, conv_st, h0, cw, cb, wa, wx, ba, bx, lam, gain):
    B, L, _ = lru3d.shape
    R1 = _row_tile(L, 768)
    W = GROUP_W
    per_b = lambda shape: pl.BlockSpec((1,) + shape, lambda b: (b,) + (0,) * len(shape))
    return pl.pallas_call(
        functools.partial(_lru_kernel, L, R1),
        grid=(B,),
        in_specs=[per_b((L, 2 * W)), per_b((3, W)), per_b((1, W)),
                  _const_spec((4, W)), _const_spec((1, W)), _const_spec((W, W)), _const_spec((W, W)),
                  _const_spec((1, W)), _const_spec((1, W)), _const_spec((1, W)), _const_spec((1, W))],
        out_specs=[per_b((L, W)), per_b((3, W)), per_b((1, W))],
        out_shape=[jax.ShapeDtypeStruct((B, L, W), bf16), jax.ShapeDtypeStruct((B, 3, W), f32),
                   jax.ShapeDtypeStruct((B, 1, W), f32)],
        scratch_shapes=[pltpu.VMEM((L + 8, W), f32), pltpu.VMEM((L, W), f32), pltpu.VMEM((L, W), f32)],
        compiler_params=_params("parallel"),
        name="lru",
    )(lru3d, conv_st, h0, cw, cb, wa, wx, ba, bx, lam, gain)


def _att_tiles(L):
    nT = L // ATT_TILE
    L0 = L - nT * ATT_TILE
    if L0 == 0:
        nT -= 1
        L0 = ATT_TILE
    return L0, nT


def _fox_kernel(L, P, *refs):
    T = ATT_TILE
    L0, nT = _att_tiles(L)
    nP = P // T
    if P:
        (q_ref, k_ref, v_ref, sm_ref, pk_ref, pv_ref, plf_ref, gain_ref,
         yb_ref, lf_ref, F_s, FT_s, acc_s, m_s, l_s) = refs
    else:
        (q_ref, k_ref, v_ref, sm_ref, gain_ref,
         yb_ref, lf_ref, F_s, FT_s, acc_s, m_s, l_s) = refs
    hm = _head_masks()
    scale = HEAD_DIM ** -0.5

    ltri = (_iota((T, T), 0) >= _iota((T, T), 1)).astype(bf16)
    sel8 = (_iota((8, SMALL_W), 0) == _iota((8, SMALL_W), 1)).astype(bf16)

    def cum_tile(lf, n, carry, f_row, ft_col):
        F = _dot_sel_lhs(ltri[:n, :n], lf) + carry
        F_s[pl.ds(f_row, n), :] = F
        FT_s[:, pl.ds(ft_col, n)] = _dot_sel_lhs(sel8, F, NT)
        return F[n - 1:n, :]

    carry = jnp.zeros((1, SMALL_W), f32)
    for j in range(nP):
        carry = cum_tile(plf_ref[0, pl.ds(j * T, T), :], T, carry, j * T, j * T)
    lf0 = _log_sigmoid(sm_ref[0, pl.ds(0, L0), :])
    lf_ref[0, pl.ds(0, L0), :] = lf0
    carry = cum_tile(lf0, L0, carry, P, P)

    def cum_body(i, carry):
        r = pl.multiple_of(L0 + i * T, 16)
        lf = _log_sigmoid(sm_ref[0, pl.ds(r, T), :])
        lf_ref[0, pl.ds(r, T), :] = lf
        return cum_tile(lf, T, carry, P + r, pl.multiple_of(P + T + i * T, T))

    lax.fori_loop(0, nT, cum_body, carry)

    def process(qoff, tq, n_full_before, is_head):
        q = q_ref[0, pl.ds(qoff, tq), :] * scale
        qh = [jnp.where(hm[h], q, 0.0).astype(bf16) for h in range(N_HEADS)]
        Fq = F_s[pl.ds(P + qoff, tq), :]
        fq = [Fq[:, h:h + 1] for h in range(N_HEADS)]
        for h in range(N_HEADS):
            m_s[h, pl.ds(0, tq), :] = jnp.full((tq, 1), NEG, f32)
            l_s[h, pl.ds(0, tq), :] = jnp.zeros((tq, 1), f32)
        acc_s[pl.ds(0, tq), :] = jnp.zeros((tq, GROUP_W), f32)

        def kv_step(kt, vt, fkt, tk, causal):
            kb = kt.astype(bf16)
            vb = vt.astype(bf16)
            acc = acc_s[pl.ds(0, tq), :]
            if causal:
                vis = _iota((tq, tk), 1) <= _iota((tq, tk), 0)
            for h in range(N_HEADS):
                s = _dot(qh[h], kb, NT) + (fq[h] - fkt[h:h + 1, :])
                if causal:
                    s = jnp.where(vis, s, NEG)
                m_old = m_s[h, pl.ds(0, tq), :]
                m_new = jnp.maximum(m_old, jnp.max(s, -1, keepdims=True))
                alpha = jnp.exp(m_old - m_new)
                p = jnp.exp(s - m_new)
                l_s[h, pl.ds(0, tq), :] = alpha * l_s[h, pl.ds(0, tq), :] + jnp.sum(p, -1, keepdims=True)
                m_s[h, pl.ds(0, tq), :] = m_new
                pv = _dot(p.astype(bf16), vb)
                acc = jnp.where(hm[h], acc * alpha + pv, acc)
            acc_s[pl.ds(0, tq), :] = acc

        if nP:
            def past_body(j, c):
                r = pl.multiple_of(j * T, T)
                kv_step(pk_ref[0, pl.ds(r, T), :], pv_ref[0, pl.ds(r, T), :], FT_s[:, pl.ds(r, T)], T, False)
                return c
            lax.fori_loop(0, nP, past_body, 0)
        if not is_head:
            kv_step(k_ref[0, pl.ds(0, L0), :], v_ref[0, pl.ds(0, L0), :], FT_s[:, pl.ds(P, L0)], L0, False)

            def full_body(j, c):
                r = pl.multiple_of(L0 + j * T, 16)
                kv_step(k_ref[0, pl.ds(r, T), :], v_ref[0, pl.ds(r, T), :],
                        FT_s[:, pl.ds(pl.multiple_of(P + T + j * T, T), T)], T, False)
                return c
            lax.fori_loop(0, n_full_before, full_body, 0)
            own_ft = FT_s[:, pl.ds(pl.multiple_of(P + T + n_full_before * T, T), T)]
        else:
            own_ft = FT_s[:, pl.ds(P, L0)]
        kv_step(k_ref[0, pl.ds(qoff, tq), :], v_ref[0, pl.ds(qoff, tq), :], own_ft, tq, True)

        l_full = l_s[N_HEADS - 1, pl.ds(0, tq), :]
        for h in range(N_HEADS - 2, -1, -1):
            l_full = jnp.where(hm[h], l_s[h, pl.ds(0, tq), :], l_full)
        out = acc_s[pl.ds(0, tq), :] / l_full
        yb_ref[0, pl.ds(qoff, tq), :] = _rms_norm(out, gain_ref[...]).astype(bf16)

    process(0, L0, 0, True)

    def q_body(i, c):
        process(pl.multiple_of(L0 + i * T, 16), T, i, False)
        return c

    lax.fori_loop(0, nT, q_body, 0)


def _fox(q, k, v, small, past_k, past_v, past_lf, gain):
    B, L, W = q.shape
    P = 0 if past_k is None else past_k.shape[1]
    T = ATT_TILE
    L0, nT = _att_tiles(L)
    per_b = lambda shape: pl.BlockSpec((1,) + shape, lambda b: (b,) + (0,) * len(shape))
    in_specs = [per_b((L, W)), per_b((L, W)), per_b((L, W)), per_b((L, SMALL_W))]
    args = [q, k, v, small]
    if P:
        in_specs += [per_b((P, W)), per_b((P, W)), per_b((P, SMALL_W))]
        args += [past_k, past_v, past_lf]
    in_specs.append(_const_spec((1, W)))
    args.append(gain)
    return pl.pallas_call(
        functools.partial(_fox_kernel, L, P),
        grid=(B,),
        in_specs=in_specs,
        out_specs=[per_b((L, W)), per_b((L, SMALL_W))],
        out_shape=[jax.ShapeDtypeStruct((B, L, W), bf16), jax.ShapeDtypeStruct((B, L, SMALL_W), f32)],
        scratch_shapes=[pltpu.VMEM((P + L, SMALL_W), f32), pltpu.VMEM((8, P + T + nT * T), f32),
                        pltpu.VMEM((T, W), f32), pltpu.VMEM((N_HEADS, T, 1), f32),
                        pltpu.VMEM((N_HEADS, T, 1), f32)],
        compiler_params=_params("parallel"),
        name="fox",
    )(*args)


def _sb_kernel(L, P, *refs):
    T = ATT_TILE
    L0, nT = _att_tiles(L)
    nP = P // T
    if P:
        q_ref, k_ref, v_ref, pk_ref, pv_ref, gain_ref, yc_ref, acc_s, c_s = refs
    else:
        q_ref, k_ref, v_ref, gain_ref, yc_ref, acc_s, c_s = refs
    hm = _head_masks()
    scale = HEAD_DIM ** -0.5

    def process(qoff, tq, n_full_before, is_head):
        q = q_ref[0, pl.ds(qoff, tq), :] * scale
        qh = [jnp.where(hm[h], q, 0.0).astype(bf16) for h in range(N_HEADS)]
        for h in range(N_HEADS):
            c_s[h, pl.ds(0, tq), :] = jnp.zeros((tq, 1), f32)
        acc_s[pl.ds(0, tq), :] = jnp.zeros((tq, GROUP_W), f32)

        def kv_step(kt, vt, tk, causal):
            kb = kt.astype(bf16)
            vb = vt.astype(bf16)
            after = (_iota((tk, tk), 0) > _iota((tk, tk), 1)).astype(bf16)
            acc = acc_s[pl.ds(0, tq), :]
            if causal:
                vis = _iota((tq, tk), 1) < _iota((tq, tk), 0)
            for h in range(N_HEADS):
                z = _dot(qh[h], kb, NT)
                ls = _log_sigmoid(z)
                lk = ls - z
                if causal:
                    lk = jnp.where(vis, lk, 0.0)
                hi, lo = _split2(lk)
                later = _dot(hi, after) + _dot(lo, after)
                c = c_s[h, pl.ds(0, tq), :]
                w = jnp.exp(ls + later + c)
                if causal:
                    w = jnp.where(vis, w, 0.0)
                c_s[h, pl.ds(0, tq), :] = c + jnp.sum(lk, -1, keepdims=True)
                acc = acc + jnp.where(hm[h], _dot(w.astype(bf16), vb), 0.0)
            acc_s[pl.ds(0, tq), :] = acc

        kv_step(k_ref[0, pl.ds(qoff, tq), :], v_ref[0, pl.ds(qoff, tq), :], tq, True)
        if not is_head:
            def full_body(t, c):
                j = n_full_before - 1 - t
                r = pl.multiple_of(L0 + j * T, 16)
                kv_step(k_ref[0, pl.ds(r, T), :], v_ref[0, pl.ds(r, T), :], T, False)
                return c
            lax.fori_loop(0, n_full_before, full_body, 0)
            kv_step(k_ref[0, pl.ds(0, L0), :], v_ref[0, pl.ds(0, L0), :], L0, False)
        if nP:
            def past_body(t, c):
                r = pl.multiple_of((nP - 1 - t) * T, T)
                kv_step(pk_ref[0, pl.ds(r, T), :], pv_ref[0, pl.ds(r, T), :], T, False)
                return c
            lax.fori_loop(0, nP, past_body, 0)

        yc_ref[0, pl.ds(qoff, tq), :] = _rms_norm(acc_s[pl.ds(0, tq), :], gain_ref[...]).astype(bf16)

    process(0, L0, 0, True)

    def q_body(i, c):
        process(pl.multiple_of(L0 + i * T, 16), T, i, False)
        return c

    lax.fori_loop(0, nT, q_body, 0)


def _sb(q, k, v, past_k, past_v, gain):
    B, L, W = q.shape
    P = 0 if past_k is None else past_k.shape[1]
    T = ATT_TILE
    per_b = lambda shape: pl.BlockSpec((1,) + shape, lambda b: (b,) + (0,) * len(shape))
    in_specs = [per_b((L, W)), per_b((L, W)), per_b((L, W))]
    args = [q, k, v]
    if P:
        in_specs += [per_b((P, W)), per_b((P, W))]
        args += [past_k, past_v]
    in_specs.append(_const_spec((1, W)))
    args.append(gain)
    return pl.pallas_call(
        functools.partial(_sb_kernel, L, P),
        grid=(B,),
        in_specs=in_specs,
        out_specs=per_b((L, W)),
        out_shape=jax.ShapeDtypeStruct((B, L, W), bf16),
        scratch_shapes=[pltpu.VMEM((T, W), f32), pltpu.VMEM((N_HEADS, T, 1), f32)],
        compiler_params=_params("parallel"),
        name="sb",
    )(*args)


def _dn_kernel(L, R1, qkv_ref, gate_ref, sm_ref, cst_ref, s0_ref, cw_ref, alog_ref, dtb_ref, ng_ref,
               yd_ref, cnew_ref, sout_ref, xpad_s, q_s, k_s, v_s, g_s, beta_s, o_s, S_s):
    W = GROUP_W
    hm = _head_masks()
    bd_ones = ((_iota((W, W), 0) >> 6) == (_iota((W, W), 1) >> 6)).astype(bf16)
    exp_g = (_iota((SMALL_W, W), 0) == (_iota((SMALL_W, W), 1) >> 6) + N_HEADS).astype(bf16)
    exp_b = (_iota((SMALL_W, W), 0) == (_iota((SMALL_W, W), 1) >> 6) + 2 * N_HEADS).astype(bf16)

    def head_sum(x):
        hi, lo = _split2(x)
        return _dot(hi, bd_ones) + _dot(lo, bd_ones)

    xpad_s[pl.ds(0, 8), :] = jnp.zeros((8, 3 * W), f32)
    xpad_s[pl.ds(5, 3), :] = cst_ref[0]
    xpad_s[pl.ds(8, L), :] = qkv_ref[0]
    cnew_ref[0] = xpad_s[pl.ds(L + 5, 3), :]
    cw = cw_ref[...]
    neg_a = -jnp.exp(alog_ref[...])
    for t in range(L // R1):
        r0 = t * R1
        rows = pl.ds(r0, R1)
        xc = cw[0:1, :] * xpad_s[pl.ds(5 + r0, R1), :]
        for i in range(1, 4):
            xc = xc + cw[i:i + 1, :] * xpad_s[pl.ds(5 + r0 + i, R1), :]
        xc = xc * jax.nn.sigmoid(xc)
        q = xc[:, 0:W]
        k = xc[:, W:2 * W]
        q_s[rows, :] = q * lax.rsqrt(head_sum(q * q) + RMS_EPS) * (HEAD_DIM ** -0.5)
        k_s[rows, :] = k * lax.rsqrt(head_sum(k * k) + RMS_EPS)
        v_s[rows, :] = xc[:, 2 * W:3 * W]
        sm = sm_ref[0, rows, :]
        g_s[rows, :] = _dot_sel_rhs(neg_a * _softplus(sm + dtb_ref[...]), exp_g)
        beta_s[rows, :] = _dot_sel_rhs(jax.nn.sigmoid(sm), exp_b)

    S_s[...] = s0_ref[0]
    bd_mask = (_iota((W, W), 0) >> 6) == (_iota((W, W), 1) >> 6)

    def chunk(off, C):
        rows = pl.ds(off, C)
        q = q_s[rows, :]
        k = k_s[rows, :]
        v = v_s[rows, :]
        beta = beta_s[rows, :]
        ri = _iota((C, C), 0)
        ci = _iota((C, C), 1)
        ltri = (ri >= ci).astype(bf16)
        gc = _dot_sel_lhs(ltri, g_s[rows, :])
        eg = jnp.exp(gc)
        g_last = gc[C - 1:C, :]
        kb = k * beta
        vb = v * beta
        kbe = kb * eg
        kbf = k.astype(bf16)
        gparts = _split3(gc)
        eye = (ri == ci).astype(f32)
        u = jnp.zeros((C, W), f32)
        w = jnp.zeros((C, W), f32)
        a_in = []
        n_sq = (C - 1).bit_length() - 1
        for h in range(N_HEADS):
            lane_h = (_iota((C, W), 1) == h * HEAD_DIM).astype(bf16)
            g_row = _dot(lane_h, gparts[0], NT) + _dot(lane_h, gparts[1], NT) + _dot(lane_h, gparts[2], NT)
            decay = jnp.exp(jnp.minimum(gc[:, h * HEAD_DIM:h * HEAD_DIM + 1] - g_row, 0.0))
            kbh = jnp.where(hm[h], kb, 0.0).astype(bf16)
            qh = jnp.where(hm[h], q, 0.0).astype(bf16)
            m = jnp.where(ri > ci, _dot(kbh, kbf, NT) * decay, 0.0)
            a_in.append(jnp.where(ri >= ci, _dot(qh, kbf, NT) * decay, 0.0).astype(bf16))
            pw = -m
            tinv = eye + pw
            for _ in range(n_sq):
                pw = _mm3(pw, pw)
                tinv = tinv + _mm3(tinv, pw)
            u = u + _mm3(tinv, jnp.where(hm[h], vb, 0.0))
            w = w + _mm3(tinv, jnp.where(hm[h], kbe, 0.0))
        S = S_s[...]
        Sb = S.astype(bf16)
        v_new = u - _dot(w.astype(bf16), Sb)
        vnb = v_new.astype(bf16)
        o = _dot((q * eg).astype(bf16), Sb)
        for h in range(N_HEADS):
            o = o + jnp.where(hm[h], _dot(a_in[h], vnb), 0.0)
        kd = (k * jnp.exp(g_last - gc)).astype(bf16)
        S_s[...] = S * jnp.exp(g_last) + jnp.where(bd_mask, _dot(kd, vnb, TN), 0.0)
        o_s[rows, :] = o

    C0 = L % DN_CHUNK
    if C0:
        chunk(0, C0)

    def chunk_body(c, carry):
        chunk(pl.multiple_of(C0 + c * DN_CHUNK, 16), DN_CHUNK)
        return carry

    lax.fori_loop(0, L // DN_CHUNK, chunk_body, 0)
    sout_ref[0] = S_s[...]

    for t in range(L // R1):
        rows = pl.ds(t * R1, R1)
        o = o_s[rows, :]
        gate = gate_ref[0, rows, :]
        y = o * lax.rsqrt(head_sum(o * o) * (1.0 / HEAD_DIM) + RMS_EPS) * ng_ref[...]
        yd_ref[0, rows, :] = (y * (gate * jax.nn.sigmoid(gate))).astype(bf16)


def _dn(qkv, gate, small, conv_st, s0_bd, cw, alog_row, dtb_row, ng_row):
    B, L, _ = qkv.shape
    W = GROUP_W
    R1 = _row_tile(L, 768)
    per_b = lambda shape: pl.BlockSpec((1,) + shape, lambda b: (b,) + (0,) * len(shape))
    return pl.pallas_call(
        functools.partial(_dn_kernel, L, R1),
        grid=(B,),
        in_specs=[per_b((L, 3 * W)), per_b((L, W)), per_b((L, SMALL_W)), per_b((3, 3 * W)), per_b((W, W)),
                  _const_spec((4, 3 * W)), _const_spec((1, SMALL_W)), _const_spec((1, SMALL_W)),
                  _const_spec((1, W))],
        out_specs=[per_b((L, W)), per_b((3, 3 * W)), per_b((W, W))],
        out_shape=[jax.ShapeDtypeStruct((B, L, W), bf16), jax.ShapeDtypeStruct((B, 3, 3 * W), f32),
                   jax.ShapeDtypeStruct((B, W, W), f32)],
        scratch_shapes=[pltpu.VMEM((L + 8, 3 * W), f32)] + [pltpu.VMEM((L, W), f32)] * 6
                       + [pltpu.VMEM((W, W), f32)],
        compiler_params=_params("parallel"),
        name="dn",
    )(qkv, gate, small, conv_st, s0_bd, cw, alog_row, dtb_row, ng_row)


def _merge_kernel(alpha, x_ref, ya_ref, yb_ref, yc_ref, yd_ref, wo_ref, g_ref, b_ref, o_ref):
    W = GROUP_W
    m = _dot(ya_ref[...], wo_ref[0:W, :])
    m = m + _dot(yb_ref[...], wo_ref[W:2 * W, :])
    m = m + _dot(yc_ref[...], wo_ref[2 * W:3 * W, :])
    m = m + _dot(yd_ref[...], wo_ref[3 * W:4 * W, :])
    o_ref[...] = _layer_norm(alpha * x_ref[...] + m, g_ref[...], b_ref[...])


def _merge(alpha, x2d, ya, yb, yc, yd, wo, g, b):
    n = x2d.shape[0]
    tm = _row_tile(n, 768)
    row = lambda wd: pl.BlockSpec((tm, wd), lambda i: (i, 0))
    return pl.pallas_call(
        functools.partial(_merge_kernel, alpha),
        grid=(n // tm,),
        in_specs=[row(D_MODEL), row(GROUP_W), row(GROUP_W), row(GROUP_W), row(GROUP_W),
                  _const_spec((D_MODEL, D_MODEL)), _const_spec((1, D_MODEL)), _const_spec((1, D_MODEL))],
        out_specs=row(D_MODEL),
        out_shape=jax.ShapeDtypeStruct((n, D_MODEL), f32),
        compiler_params=_params("parallel"),
        name="merge",
    )(x2d, ya, yb, yc, yd, wo, g, b)


def _ffn_kernel(alpha, Bt, Lt, F, n_chunks, x_ref, st_ref, win_ref, cw_ref, cb_ref, wout_ref, g_ref, b_ref,
                y_ref, nst_ref, carry_s):
    R = Bt * Lt
    Fc = F // n_chunks

    @pl.when(pl.program_id(1) == 0)
    def _():
        carry_s[...] = st_ref[...]

    if Bt == 1:
        x = x_ref[0]
    else:
        x = jnp.concatenate([x_ref[s] for s in range(Bt)], axis=0)
    xb = x.astype(bf16)
    row_in = _iota((R, 1), 0)
    if Bt > 1:
        assert Lt & (Lt - 1) == 0
        row_in = row_in & (Lt - 1)
    first = row_in == 0
    second = row_in == 1

    def rows_of(prev_row, c0):
        parts = [jnp.broadcast_to(carry_s[s, prev_row:prev_row + 1, c0:c0 + Fc], (Lt, Fc)) for s in range(Bt)]
        return parts[0] if Bt == 1 else jnp.concatenate(parts, axis=0)

    acc = jnp.zeros((R, D_MODEL), f32)
    for c in range(n_chunks):
        c0 = c * Fc
        gt = _dot(xb, win_ref[:, c0:c0 + Fc])
        up = _dot(xb, win_ref[:, F + c0:F + c0 + Fc])
        p0 = rows_of(0, c0)
        p1 = rows_of(1, c0)
        g1 = jnp.where(first, p1, pltpu.roll(gt, 1, 0))
        g2 = jnp.where(first, p0, jnp.where(second, p1, pltpu.roll(gt, 2, 0)))
        conv = (cw_ref[0:1, c0:c0 + Fc] * g2 + cw_ref[1:2, c0:c0 + Fc] * g1
                + cw_ref[2:3, c0:c0 + Fc] * gt + cb_ref[:, c0:c0 + Fc])
        hid = (jax.nn.gelu(conv) * up).astype(bf16)
        acc = acc + _dot(hid, wout_ref[c0:c0 + Fc, :])
        for s in range(Bt):
            carry_s[s, :, c0:c0 + Fc] = gt[s * Lt + Lt - 2:s * Lt + Lt, :]
    y = _layer_norm(alpha * x + acc, g_ref[...], b_ref[...])
    for s in range(Bt):
        y_ref[s] = y[s * Lt:(s + 1) * Lt, :]
    nst_ref[...] = carry_s[...]


def _ffn(alpha, x3d, st, win, cw, cb, wout, g, b):
    B, L, _ = x3d.shape
    F = wout.shape[0]
    if L >= 256:
        Bt, Lt = 1, _row_tile(L, 512, 8)
    else:
        Bt, Lt = B, L
    n_chunks = 2
    blk = lambda shape: pl.BlockSpec((Bt,) + shape, lambda i, t: (i, t) + (0,) * (len(shape) - 1))
    st_spec = pl.BlockSpec((Bt, 2, F), lambda i, t: (i, 0, 0))
    return pl.pallas_call(
        functools.partial(_ffn_kernel, alpha, Bt, Lt, F, n_chunks),
        grid=(B // Bt, L // Lt),
        in_specs=[blk((Lt, D_MODEL)), st_spec, _const_spec((D_MODEL, 2 * F)), _const_spec((3, F)),
                  _const_spec((1, F)), _const_spec((F, D_MODEL)), _const_spec((1, D_MODEL)),
                  _const_spec((1, D_MODEL))],
        out_specs=[blk((Lt, D_MODEL)), st_spec],
        out_shape=[jax.ShapeDtypeStruct((B, L, D_MODEL), f32), jax.ShapeDtypeStruct((B, 2, F), f32)],
        scratch_shapes=[pltpu.VMEM((Bt, 2, F), f32)],
        compiler_params=_params("parallel", "arbitrary"),
        name="ffn",
    )(x3d, st, win, cw, cb, wout, g, b)


def _prep_layer_weights(l, P):
    W = GROUP_W
    w_in, b_in = P['w_in'][l], P['b_in'][l]
    o_fox = 2 * W
    o_ff = o_fox + 3 * W
    o_sb = o_ff + N_HEADS
    o_dn = o_sb + 3 * W
    o_da = o_dn + 3 * W
    o_db = o_da + N_HEADS
    o_dg = o_db + N_HEADS
    pad = SMALL_W - 3 * N_HEADS

    def reorder(a):
        parts = [a[..., 0:o_ff], a[..., o_sb:o_da], a[..., o_dg:o_dg + W], a[..., o_ff:o_ff + N_HEADS],
                 a[..., o_da:o_da + 2 * N_HEADS], jnp.zeros(a.shape[:-1] + (pad,), a.dtype)]
        return jnp.concatenate(parts, axis=-1)

    w_r = reorder(w_in).astype(bf16)
    b_r = reorder(b_in).reshape(1, -1)

    def block_diag(w4):
        eye = jnp.eye(N_HEADS, dtype=w4.dtype)
        return (w4[:, :, None, :] * eye[:, None, :, None]).reshape(W, W)

    small_row = lambda vals, off: jnp.zeros((1, SMALL_W), f32).at[0, off:off + N_HEADS].set(vals)
    r1 = lambda a: a.reshape(1, -1)
    return dict(
        w_in=w_r, b_in=b_r,
        lru_cw=P['lru_conv_w'][l], lru_cb=r1(P['lru_conv_b'][l]),
        lru_wa=block_diag(P['lru_w_a'][l]).astype(bf16), lru_wx=block_diag(P['lru_w_x'][l]).astype(bf16),
        lru_ba=r1(P['lru_b_a'][l]), lru_bx=r1(P['lru_b_x'][l]), lru_lam=r1(P['lru_lambda'][l]),
        gains=[r1(P['grp_norm_g'][l][i]) for i in range(3)],
        dn_cw=P['dn_conv_w'][l], dn_alog=small_row(P['dn_a_log'][l], N_HEADS),
        dn_dtb=small_row(P['dn_dt_bias'][l], N_HEADS), dn_ng=r1(jnp.tile(P['dn_norm_g'][l], N_HEADS)),
        w_o=P['w_o'][l].astype(bf16), ln1_g=r1(P['ln1_g'][l]), ln1_b=r1(P['ln1_b'][l]),
        ffn_win=P['ffn_w_in'][l].astype(bf16), ffn_cw=P['ffn_conv_w'][l], ffn_cb=r1(P['ffn_conv_b'][l]),
        ffn_wout=P['ffn_w_out'][l].astype(bf16), ln2_g=r1(P['ln2_g'][l]), ln2_b=r1(P['ln2_b'][l]),
    )


def _embed_block_diag(S):
    B = S.shape[0]
    eye = jnp.eye(N_HEADS, dtype=S.dtype)
    return (S[:, :, :, None, :] * eye[None, :, None, :, None]).reshape(B, GROUP_W, GROUP_W)


def _extract_block_diag(Sbd):
    return jnp.stack([Sbd[:, h * HEAD_DIM:(h + 1) * HEAD_DIM, h * HEAD_DIM:(h + 1) * HEAD_DIM]
                      for h in range(N_HEADS)], axis=1)


def _trunk(x, st, LW, ln_in, alpha):
    B, L, D = x.shape
    W = GROUP_W
    F = LW[0]['ffn_wout'].shape[0]
    new = {n: [] for n in ('lru_conv', 'lru_h', 'fox_k', 'fox_v', 'fox_logf', 'sb_k', 'sb_v',
                           'dn_conv', 'dn_S', 'ffn_conv')}
    x2 = x.reshape(B * L, D)
    for l, w in enumerate(LW):
        outs = _proj(x2, ln_in[0], ln_in[1], w['w_in'], w['b_in'], do_ln=(l == 0))
        if l == 0:
            x2, outs = outs[0], outs[1:]
        lru, fq, fk, fv, sq, sk, sv, dqkv, dgate, small = [o.reshape(B, L, -1) for o in outs]
        if st is None:
            lru_conv = jnp.zeros((B, 3, W), f32)
            lru_h = jnp.zeros((B, 1, W), f32)
            dn_conv = jnp.zeros((B, 3, 3 * W), f32)
            dn_S = jnp.zeros((B, W, W), f32)
            ffn_conv = jnp.zeros((B, 2, F), f32)
            pfk = pfv = plf = psk = psv = None
        else:
            Pn = st['fox_k'].shape[2]
            lru_conv = st['lru_conv'][l]
            lru_h = st['lru_h'][l].reshape(B, 1, W)
            dn_conv = st['dn_conv'][l]
            dn_S = _embed_block_diag(st['dn_S'][l])
            ffn_conv = st['ffn_conv'][l]
            pfk = st['fox_k'][l].reshape(B, Pn, W)
            pfv = st['fox_v'][l].reshape(B, Pn, W)
            plf = jnp.pad(st['fox_logf'][l], ((0, 0), (0, 0), (0, SMALL_W - N_HEADS)))
            psk = st['sb_k'][l].reshape(B, Pn, W)
            psv = st['sb_v'][l].reshape(B, Pn, W)

        ya, lru_conv_new, h_last = _lru(lru, lru_conv, lru_h, w['lru_cw'], w['lru_cb'], w['lru_wa'],
                                        w['lru_wx'], w['lru_ba'], w['lru_bx'], w['lru_lam'], w['gains'][0])
        yb, logf = _fox(fq, fk, fv, small, pfk, pfv, plf, w['gains'][1])
        yc = _sb(sq, sk, sv, psk, psv, w['gains'][2])
        yd, dn_conv_new, S_new = _dn(dqkv, dgate, small, dn_conv, dn_S, w['dn_cw'], w['dn_alog'],
                                     w['dn_dtb'], w['dn_ng'])
        x1 = _merge(alpha, x2, ya.reshape(B * L, W), yb.reshape(B * L, W), yc.reshape(B * L, W),
                    yd.reshape(B * L, W), w['w_o'], w['ln1_g'], w['ln1_b'])
        x3, ffn_new = _ffn(alpha, x1.reshape(B, L, D), ffn_conv, w['ffn_win'], w['ffn_cw'], w['ffn_cb'],
                           w['ffn_wout'], w['ln2_g'], w['ln2_b'])
        x2 = x3.reshape(B * L, D)

        heads = lambda a: a.reshape(B, L, N_HEADS, HEAD_DIM)
        new['lru_conv'].append(lru_conv_new)
        new['lru_h'].append(h_last.reshape(B, W))
        new['fox_k'].append(heads(fk))
        new['fox_v'].append(heads(fv))
        new['fox_logf'].append(logf[:, :, :N_HEADS])
        new['sb_k'].append(heads(sk))
        new['sb_v'].append(heads(sv))
        new['dn_conv'].append(dn_conv_new)
        new['dn_S'].append(_extract_block_diag(S_new))
        new['ffn_conv'].append(ffn_new)
    return x2.reshape(B, L, D), {n: jnp.stack(v) for n, v in new.items()}


def kernel(x_prompt, x_sample, state_lru_conv, state_lru_h, cache_fox_k, cache_fox_v, cache_fox_logf,
           cache_sb_k, cache_sb_v, state_dn_conv, state_dn_S, state_ffn_conv, meta_tokens, ln_in_g, ln_in_b,
           w_in, b_in, lru_conv_w, lru_conv_b, lru_w_a, lru_b_a, lru_w_x, lru_b_x, lru_lambda, dn_conv_w,
           dn_a_log, dn_dt_bias, dn_norm_g, grp_norm_g, w_o, ln1_g, ln1_b, ffn_w_in, ffn_conv_w, ffn_conv_b,
           ffn_w_out, ln2_g, ln2_b):
    P = dict(w_in=w_in, b_in=b_in, lru_conv_w=lru_conv_w, lru_conv_b=lru_conv_b, lru_w_a=lru_w_a,
             lru_b_a=lru_b_a, lru_w_x=lru_w_x, lru_b_x=lru_b_x, lru_lambda=lru_lambda, dn_conv_w=dn_conv_w,
             dn_a_log=dn_a_log, dn_dt_bias=dn_dt_bias, dn_norm_g=dn_norm_g, grp_norm_g=grp_norm_g, w_o=w_o,
             ln1_g=ln1_g, ln1_b=ln1_b, ffn_w_in=ffn_w_in, ffn_conv_w=ffn_conv_w, ffn_conv_b=ffn_conv_b,
             ffn_w_out=ffn_w_out, ln2_g=ln2_g, ln2_b=ln2_b)
    depth = w_in.shape[0]
    assert x_prompt.shape[2] == D_MODEL and cache_fox_k.shape[2] % ATT_TILE == 0
    alpha = (2.0 * depth) ** 0.25
    LW = [_prep_layer_weights(l, P) for l in range(depth)]
    ln_in = (ln_in_g.reshape(1, -1), ln_in_b.reshape(1, -1))

    bp = x_prompt.shape[0]
    meta = jnp.broadcast_to(meta_tokens.astype(x_prompt.dtype), (bp, N_META, D_MODEL))
    xp = jnp.concatenate([meta, x_prompt], axis=1)
    yp, ps = _trunk(xp, None, LW, ln_in, alpha)

    st_in = dict(lru_conv=state_lru_conv, lru_h=state_lru_h, fox_k=cache_fox_k, fox_v=cache_fox_v,
                 fox_logf=cache_fox_logf, sb_k=cache_sb_k, sb_v=cache_sb_v, dn_conv=state_dn_conv,
                 dn_S=state_dn_S, ffn_conv=state_ffn_conv)
    ys, ss = _trunk(x_sample, st_in, LW, ln_in, alpha)

    names = ('lru_conv', 'lru_h', 'fox_k', 'fox_v', 'fox_logf', 'sb_k', 'sb_v', 'dn_conv', 'dn_S', 'ffn_conv')
    return (yp[:, N_META:], ys) + tuple(ps[n] for n in names) + tuple(ss[n] for n in names)
```

```python
import functools

import jax
import jax.numpy as jnp
from jax import lax
from jax.experimental import pallas as pl
from jax.experimental.pallas import tpu as pltpu

f32 = jnp.float32
bf16 = jnp.bfloat16

D_MODEL = 1024
GROUP_W = 256
N_HEADS = 4
HEAD_DIM = 64
N_META = 16
LRU_C = 8.0
LN_EPS = 1e-5
RMS_EPS = 1e-6
ATT_TILE = 256
ATT_SUB = 128
DN_CHUNK = 64
NEG = -1e30
LOG2E = 1.4426950408889634
SMALL_W = 128
VMEM_LIMIT_BYTES = 56 * 1024 * 1024

NN = (((1,), (0,)), ((), ()))
NT = (((1,), (1,)), ((), ()))
TN = (((0,), (0,)), ((), ()))


def _dot(a, b, dims=NN):
    return lax.dot_general(a, b, dims, preferred_element_type=f32)


def _split2(x):
    hi = x.astype(bf16)
    lo = (x - hi.astype(f32)).astype(bf16)
    return hi, lo


def _split3(x):
    p0 = x.astype(bf16)
    r = x - p0.astype(f32)
    p1 = r.astype(bf16)
    p2 = (r - p1.astype(f32)).astype(bf16)
    return p0, p1, p2


def _dot_sel_rhs(x, sel, dims=NN):
    p0, p1, p2 = _split3(x)
    return _dot(p0, sel, dims) + _dot(p1, sel, dims) + _dot(p2, sel, dims)


def _dot_sel_lhs(sel, x, dims=NN):
    p0, p1, p2 = _split3(x)
    return _dot(sel, p0, dims) + _dot(sel, p1, dims) + _dot(sel, p2, dims)


def _mm3(a, b):
    ah, al = _split2(a)
    bh, bl = _split2(b)
    return _dot(ah, bh) + _dot(ah, bl) + _dot(al, bh)


def _iota(shape, dim):
    return lax.broadcasted_iota(jnp.int32, shape, dim)


def _head_masks(width=GROUP_W):
    lane_head = _iota((1, width), 1) >> 6
    return [lane_head == h for h in range(N_HEADS)]


def _softplus(x):
    return jnp.maximum(x, 0.0) + jnp.log1p(jnp.exp(-jnp.abs(x)))


def _log_sigmoid(x):
    return jnp.minimum(x, 0.0) - jnp.log1p(jnp.exp(-jnp.abs(x)))


def _layer_norm(x, g, b):
    mu = jnp.mean(x, -1, keepdims=True)
    xc = x - mu
    var = jnp.mean(xc * xc, -1, keepdims=True)
    return xc * lax.rsqrt(var + LN_EPS) * g + b


def _rms_norm(x, g):
    return x * lax.rsqrt(jnp.mean(x * x, -1, keepdims=True) + RMS_EPS) * g


def _row_tile(n, cap, mult=16):
    best = None
    for d in range(mult, min(n, cap) + 1, mult):
        if n % d == 0:
            best = d
    return best if best is not None else n


def _const_spec(shape):
    nd = len(shape)
    return pl.BlockSpec(shape, lambda *_: (0,) * nd, pipeline_mode=pl.Buffered(1))


def _params(*sem):
    return pltpu.CompilerParams(dimension_semantics=sem, vmem_limit_bytes=VMEM_LIMIT_BYTES)


PROJ_WIDTHS = (512, 256, 256, 256, 256, 256, 256, 768, 256, SMALL_W)


def _proj_kernel(do_ln, x_ref, g_ref, b_ref, w_ref, bias_ref, *outs):
    x = x_ref[...]
    if do_ln:
        x = _layer_norm(x, g_ref[...], b_ref[...])
        outs[0][...] = x
        outs = outs[1:]
    xb = x.astype(bf16)
    col = 0
    for o, wd in zip(outs, PROJ_WIDTHS):
        o[...] = _dot(xb, w_ref[:, col:col + wd]) + bias_ref[:, col:col + wd]
        col += wd


def _proj(x2d, ln_g, ln_b, w, bias, do_ln):
    n = x2d.shape[0]
    tm = _row_tile(n, 768)
    wtot = w.shape[1]
    row = lambda wd: pl.BlockSpec((tm, wd), lambda i: (i, 0))
    out_shape = [jax.ShapeDtypeStruct((n, wd), f32) for wd in PROJ_WIDTHS]
    out_specs = [row(wd) for wd in PROJ_WIDTHS]
    if do_ln:
        out_shape = [jax.ShapeDtypeStruct((n, D_MODEL), f32)] + out_shape
        out_specs = [row(D_MODEL)] + out_specs
    return pl.pallas_call(
        functools.partial(_proj_kernel, do_ln),
        grid=(n // tm,),
        in_specs=[row(D_MODEL), _const_spec((1, D_MODEL)), _const_spec((1, D_MODEL)),
                  _const_spec((D_MODEL, wtot)), _const_spec((1, wtot))],
        out_specs=out_specs,
        out_shape=out_shape,
        compiler_params=_params("parallel"),
        name="proj",
    )(x2d, ln_g, ln_b, w, bias)


def _lru_kernel(L, R1, lru_ref, cst_ref, h0_ref, cw_ref, cb_ref, wa_ref, wx_ref, ba_ref, bx_ref,
                lam_ref, gain_ref, ya_ref, cnew_ref, hlast_ref, xpad_s, a_s, b_s):
    W = GROUP_W
    xpad_s[pl.ds(0, 8), :] = jnp.zeros((8, W), f32)
    xpad_s[pl.ds(5, 3), :] = cst_ref[0]
    xpad_s[pl.ds(8, L), :] = lru_ref[0, :, 0:W]
    cnew_ref[0] = xpad_s[pl.ds(L + 5, 3), :]

    sp = _softplus(-lam_ref[...])
    cw = cw_ref[...]
    for t in range(L // R1):
        r0 = t * R1
        xa = cb_ref[...]
        for i in range(4):
            xa = xa + cw[i:i + 1, :] * xpad_s[pl.ds(5 + r0 + i, R1), :]
        xb = xa.astype(bf16)
        r = jax.nn.sigmoid(_dot(xb, wa_ref[...]) + ba_ref[...])
        ig = jax.nn.sigmoid(_dot(xb, wx_ref[...]) + bx_ref[...])
        log_a = (-LRU_C) * r * sp
        y2 = 2.0 * log_a
        one_m_a2 = jnp.tanh(-0.5 * y2) * (1.0 + jnp.exp(y2))
        a_s[pl.ds(r0, R1), :] = jnp.exp(log_a)
        b_s[pl.ds(r0, R1), :] = jnp.sqrt(one_m_a2) * (ig * xa)

    rows8 = _iota((8, W), 0)

    def scan_body(g, hprev):
        off = pl.multiple_of(g * 8, 8)
        A = a_s[pl.ds(off, 8), :]
        Bv = b_s[pl.ds(off, 8), :]
        for s in (1, 2, 4):
            keep = rows8 >= s
            a_sh = jnp.where(keep, pltpu.roll(A, s, 0), 1.0)
            b_sh = jnp.where(keep, pltpu.roll(Bv, s, 0), 0.0)
            Bv = A * b_sh + Bv
            A = A * a_sh
        h = A * hprev + Bv
        b_s[pl.ds(off, 8), :] = h
        return h[7:8, :]

    hlast_ref[0] = lax.fori_loop(0, L // 8, scan_body, h0_ref[0])

    for t in range(L // R1):
        r0 = t * R1
        o = jax.nn.gelu(lru_ref[0, pl.ds(r0, R1), W:2 * W]) * b_s[pl.ds(r0, R1), :]
        ya_ref[0, pl.ds(r0, R1), :] = _rms_norm(o, gain_ref[...]).astype(bf16)


def _lru(lru3d, conv_st, h0, cw, cb, wa, wx, ba, bx, lam, gain):
    B, L, _ = lru3d.shape
    R1 = _row_tile(L, 768)
    W = GROUP_W
    per_b = lambda shape: pl.BlockSpec((1,) + shape, lambda b: (b,) + (0,) * len(shape))
    return pl.pallas_call(
        functools.partial(_lru_kernel, L, R1),
        grid=(B,),
        in_specs=[per_b((L, 2 * W)), per_b((3, W)), per_b((1, W)),
                  _const_spec((4, W)), _const_spec((1, W)), _const_spec((W, W)), _const_spec((W, W)),
                  _const_spec((1, W)), _const_spec((1, W)), _const_spec((1, W)), _const_spec((1, W))],
        out_specs=[per_b((L, W)), per_b((3, W)), per_b((1, W))],
        out_shape=[jax.ShapeDtypeStruct((B, L, W), bf16), jax.ShapeDtypeStruct((B, 3, W), f32),
                   jax.ShapeDtypeStruct((B, 1, W), f32)],
        scratch_shapes=[pltpu.VMEM((L + 8, W), f32), pltpu.VMEM((L, W), f32), pltpu.VMEM((L, W), f32)],
        compiler_params=_params("parallel"),
        name="lru",
    )(lru3d, conv_st, h0, cw, cb, wa, wx, ba, bx, lam, gain)


def _att_layout(L, P):
    T = ATT_TILE
    assert (P + L) % 16 == 0
    front = (-(P + L)) % T
    Lpad = front + P + L
    L0 = L % T
    nT = L // T
    base_t = (front + P + L0) // T
    return front, Lpad, L0, nT, base_t


def _stage_pairs(front, P, L, new_ref, past_ref, flat_s, pair_s, hm):
    W = GROUP_W
    S = ATT_SUB
    if front:
        flat_s[pl.ds(0, front), :] = jnp.zeros((front, W), bf16)
    if P:
        flat_s[pl.ds(front, P), :] = past_ref[0].astype(bf16)
    flat_s[pl.ds(front + P, L), :] = new_ref[0].astype(bf16)
    zero = jnp.zeros((S, W), bf16)

    def body(u, c):
        x = flat_s[pl.ds(pl.multiple_of(u * S, S), S), :]
        base = pl.multiple_of(u * 2 * S, 2 * S)
        for p in range(2):
            pair_s[p, pl.ds(base, S), :] = jnp.where(hm[2 * p], x, zero)
            pair_s[p, pl.ds(base + S, S), :] = jnp.where(hm[2 * p + 1], x, zero)
        return c

    lax.fori_loop(0, (front + P + L) // S, body, 0)


def _pair_scores(qb, kpair_s, j):
    S2 = 2 * ATT_SUB
    kbase = j * (2 * ATT_TILE)
    return [[_dot(qb, kpair_s[p, pl.ds(pl.multiple_of(kbase + sub * S2, S2), S2), :], NT)
             for sub in range(2)] for p in range(2)]


def _pair_pv(wb, vpair_s, j):
    S2 = 2 * ATT_SUB
    kbase = j * (2 * ATT_TILE)
    out = None
    for sub in range(2):
        for p in range(2):
            lhs = jnp.concatenate([wb[2 * p][sub], wb[2 * p + 1][sub]], axis=1)
            t = _dot(lhs, vpair_s[p, pl.ds(pl.multiple_of(kbase + sub * S2, S2), S2), :])
            out = t if out is None else out + t
    return out


def _by_head_lanes(vals, lo_half):
    return jnp.concatenate([jnp.where(lo_half, vals[0], vals[1]), jnp.where(lo_half, vals[2], vals[3])], axis=1)


def _fox_kernel(L, P, *refs):
    T, S = ATT_TILE, ATT_SUB
    front, Lpad, L0, nT, base_t = _att_layout(L, P)
    if P:
        q_ref, k_ref, v_ref, sm_ref, pk_ref, pv_ref, plf_ref, gain_ref, yb_ref, lf_ref = refs[:10]
        scr = refs[10:]
    else:
        q_ref, k_ref, v_ref, sm_ref, gain_ref, yb_ref, lf_ref = refs[:7]
        pk_ref = pv_ref = plf_ref = None
        scr = refs[7:]
    flat_s, kpair_s, vpair_s, lfp_s, F_s, FT_s, qb_s, fqb_s, m_s, lp_s, acc_s = scr
    hm = _head_masks()
    lo_half = _iota((1, S), 1) < HEAD_DIM
    scale = HEAD_DIM ** -0.5 * LOG2E

    _stage_pairs(front, P, L, k_ref, pk_ref, flat_s, kpair_s, hm)
    _stage_pairs(front, P, L, v_ref, pv_ref, flat_s, vpair_s, hm)

    if front:
        lfp_s[pl.ds(0, front), :] = jnp.zeros((front, SMALL_W), f32)
    if P:
        lfp_s[pl.ds(front, P), :] = plf_ref[0]
    lf = _log_sigmoid(sm_ref[0])
    lf_ref[0] = lf
    lfp_s[pl.ds(front + P, L), :] = lf
    ltri = (_iota((S, S), 0) >= _iota((S, S), 1)).astype(bf16)
    sel8 = (_iota((8, SMALL_W), 0) == _iota((8, SMALL_W), 1)).astype(bf16)

    def cum_body(u, carry):
        r = pl.multiple_of(u * S, S)
        F = _dot_sel_lhs(ltri, lfp_s[pl.ds(r, S), :]) + carry
        F2 = F * LOG2E
        F_s[pl.ds(r, S), :] = F2
        FT_s[:, pl.ds(r, S)] = _dot_sel_lhs(sel8, F2, NT)
        return F[S - 1:S, :]

    lax.fori_loop(0, Lpad // S, cum_body, jnp.zeros((1, SMALL_W), f32))

    def process(qnat, tq, d):
        rows = pl.ds(0, tq)
        qp0 = front + P + qnat
        qb_s[rows, :] = (q_ref[0, pl.ds(qnat, tq), :] * scale).astype(bf16)
        Fq = F_s[pl.ds(qp0, tq), :]
        for h in range(N_HEADS):
            fqb_s[h, rows, :] = jnp.broadcast_to(Fq[:, h:h + 1], (tq, S))
            m_s[h, rows, :] = jnp.full((tq, S), NEG, f32)
            lp_s[h, rows, :] = jnp.zeros((tq, S), f32)
        acc_s[rows, :] = jnp.zeros((tq, GROUP_W), f32)

        def kv_step(j, masked):
            sc = _pair_scores(qb_s[rows, :], kpair_s, j)
            if masked:
                qpos = qp0 + _iota((tq, S), 0)
                vis = []
                for sub in range(2):
                    kpos = j * T + sub * S + _iota((tq, S), 1)
                    ok = kpos <= qpos
                    if front:
                        ok = ok & (kpos >= front)
                    vis.append(ok)
            alphas, pb = [], []
            for h in range(N_HEADS):
                p_, half = divmod(h, 2)
                fq = fqb_s[h, rows, :]
                c = []
                for sub in range(2):
                    fk = FT_s[h:h + 1, pl.ds(pl.multiple_of(j * T + sub * S, S), S)]
                    x = sc[p_][sub][:, half * S:(half + 1) * S] + (fq - fk)
                    if masked:
                        x = jnp.where(vis[sub], x, NEG)
                    c.append(x)
                m_old = m_s[h, rows, :]
                m_new = jnp.maximum(m_old, jnp.max(jnp.maximum(c[0], c[1]), -1, keepdims=True))
                alpha = jnp.exp2(m_old - m_new)
                e0 = jnp.exp2(c[0] - m_new)
                e1 = jnp.exp2(c[1] - m_new)
                lp_s[h, rows, :] = alpha * lp_s[h, rows, :] + (e0 + e1)
                m_s[h, rows, :] = m_new
                alphas.append(alpha)
                pb.append((e0.astype(bf16), e1.astype(bf16)))
            pv = _pair_pv(pb, vpair_s, j)
            acc_s[rows, :] = acc_s[rows, :] * _by_head_lanes(alphas, lo_half) + pv

        def plain(j, c):
            kv_step(j, False)
            return c

        jf = 1 if front else 0
        if isinstance(d, int):
            if d > 0:
                if front:
                    kv_step(0, True)
                lax.fori_loop(jf, d, plain, 0)
        else:
            if front:
                kv_step(0, True)
            lax.fori_loop(jf, d, plain, 0)
        kv_step(d, True)

        l = [jnp.sum(lp_s[h, rows, :], -1, keepdims=True) for h in range(N_HEADS)]
        out = acc_s[rows, :] / _by_head_lanes(l, lo_half)
        yb_ref[0, pl.ds(qnat, tq), :] = _rms_norm(out, gain_ref[...]).astype(bf16)

    if L0:
        process(0, L0, base_t - 1)

    def q_body(i, c):
        process(pl.multiple_of(L0 + i * T, 16), T, base_t + i)
        return c

    lax.fori_loop(0, nT, q_body, 0)


def _fox(q, k, v, small, past_k, past_v, past_lf, gain):
    B, L, W = q.shape
    P = 0 if past_k is None else past_k.shape[1]
    T = ATT_TILE
    _, Lpad, _, _, _ = _att_layout(L, P)
    per_b = lambda shape: pl.BlockSpec((1,) + shape, lambda b: (b,) + (0,) * len(shape))
    in_specs = [per_b((L, W)), per_b((L, W)), per_b((L, W)), per_b((L, SMALL_W))]
    args = [q, k, v, small]
    if P:
        in_specs += [per_b((P, W)), per_b((P, W)), per_b((P, SMALL_W))]
        args += [past_k, past_v, past_lf]
    in_specs.append(_const_spec((1, W)))
    args.append(gain)
    return pl.pallas_call(
        functools.partial(_fox_kernel, L, P),
        grid=(B,),
        in_specs=in_specs,
        out_specs=[per_b((L, W)), per_b((L, SMALL_W))],
        out_shape=[jax.ShapeDtypeStruct((B, L, W), bf16), jax.ShapeDtypeStruct((B, L, SMALL_W), f32)],
        scratch_shapes=[pltpu.VMEM((Lpad, W), bf16), pltpu.VMEM((2, 2 * Lpad, W), bf16),
                        pltpu.VMEM((2, 2 * Lpad, W), bf16), pltpu.VMEM((Lpad, SMALL_W), f32),
                        pltpu.VMEM((Lpad, SMALL_W), f32), pltpu.VMEM((8, Lpad), f32),
                        pltpu.VMEM((T, W), bf16), pltpu.VMEM((N_HEADS, T, ATT_SUB), f32),
                        pltpu.VMEM((N_HEADS, T, ATT_SUB), f32), pltpu.VMEM((N_HEADS, T, ATT_SUB), f32),
                        pltpu.VMEM((T, W), f32)],
        compiler_params=_params("parallel"),
        name="fox",
    )(*args)


def _sb_kernel(L, P, *refs):
    T, S = ATT_TILE, ATT_SUB
    front, Lpad, L0, nT, base_t = _att_layout(L, P)
    if P:
        q_ref, k_ref, v_ref, pk_ref, pv_ref, gain_ref, yc_ref = refs[:7]
        scr = refs[7:]
    else:
        q_ref, k_ref, v_ref, gain_ref, yc_ref = refs[:5]
        pk_ref = pv_ref = None
        scr = refs[5:]
    flat_s, kpair_s, vpair_s, qb_s, c_s, acc_s = scr
    hm = _head_masks()
    scale = HEAD_DIM ** -0.5

    _stage_pairs(front, P, L, k_ref, pk_ref, flat_s, kpair_s, hm)
    _stage_pairs(front, P, L, v_ref, pv_ref, flat_s, vpair_s, hm)

    after2 = ((_iota((2 * T, T), 0) & (T - 1)) > _iota((2 * T, T), 1)).astype(bf16)

    def process(qnat, tq, d):
        rows = pl.ds(0, tq)
        qp0 = front + P + qnat
        qb_s[rows, :] = (q_ref[0, pl.ds(qnat, tq), :] * scale).astype(bf16)
        for h in range(N_HEADS):
            c_s[h, rows, :] = jnp.zeros((tq, S), f32)
        acc_s[rows, :] = jnp.zeros((tq, GROUP_W), f32)

        def kv_step(j, masked):
            sc = _pair_scores(qb_s[rows, :], kpair_s, j)
            if masked:
                kpos = j * T + _iota((tq, T), 1)
                vis = kpos < qp0 + _iota((tq, T), 0)
                if front:
                    vis = vis & (kpos >= front)
            wb = []
            for h in range(N_HEADS):
                p_, half = divmod(h, 2)
                z = jnp.concatenate([sc[p_][sub][:, half * S:(half + 1) * S] for sub in range(2)], axis=1)
                ls = jnp.minimum(z, 0.0) - jnp.log(1.0 + jnp.exp(-jnp.abs(z)))
                lk = ls - z
                if masked:
                    lk = jnp.where(vis, lk, 0.0)
                hi, lo = _split2(lk)
                later = _dot(jnp.concatenate([hi, lo], axis=1), after2)
                c = c_s[h, rows, :]
                w = jnp.exp(ls + later + jnp.concatenate([c, c], axis=1))
                if masked:
                    w = jnp.where(vis, w, 0.0)
                c_s[h, rows, :] = c + jnp.sum(lk, -1, keepdims=True)
                wbf = w.astype(bf16)
                wb.append((wbf[:, 0:S], wbf[:, S:2 * S]))
            acc_s[rows, :] = acc_s[rows, :] + _pair_pv(wb, vpair_s, j)

        jf = 1 if front else 0
        kv_step(d, True)

        def plain(t, c):
            kv_step(d - 1 - t, False)
            return c

        if isinstance(d, int):
            if d > 0:
                lax.fori_loop(0, d - jf, plain, 0)
                if front:
                    kv_step(0, True)
        else:
            lax.fori_loop(0, d - jf, plain, 0)
            if front:
                kv_step(0, True)

        yc_ref[0, pl.ds(qnat, tq), :] = _rms_norm(acc_s[rows, :], gain_ref[...]).astype(bf16)

    if L0:
        process(0, L0, base_t - 1)

    def q_body(i, c):
        process(pl.multiple_of(L0 + i * T, 16), T, base_t + i)
        return c

    lax.fori_loop(0, nT, q_body, 0)


def _sb(q, k, v, past_k, past_v, gain):
    B, L, W = q.shape
    P = 0 if past_k is None else past_k.shape[1]
    T = ATT_TILE
    _, Lpad, _, _, _ = _att_layout(L, P)
    per_b = lambda shape: pl.BlockSpec((1,) + shape, lambda b: (b,) + (0,) * len(shape))
    in_specs = [per_b((L, W)), per_b((L, W)), per_b((L, W))]
    args = [q, k, v]
    if P:
        in_specs += [per_b((P, W)), per_b((P, W))]
        args += [past_k, past_v]
    in_specs.append(_const_spec((1, W)))
    args.append(gain)
    return pl.pallas_call(
        functools.partial(_sb_kernel, L, P),
        grid=(B,),
        in_specs=in_specs,
        out_specs=per_b((L, W)),
        out_shape=jax.ShapeDtypeStruct((B, L, W), bf16),
        scratch_shapes=[pltpu.VMEM((Lpad, W), bf16), pltpu.VMEM((2, 2 * Lpad, W), bf16),
                        pltpu.VMEM((2, 2 * Lpad, W), bf16), pltpu.VMEM((T, W), bf16),
                        pltpu.VMEM((N_HEADS, T, ATT_SUB), f32), pltpu.VMEM((T, W), f32)],
        compiler_params=_params("parallel"),
        name="sb",
    )(*args)


def _dn_kernel(L, R1, qkv_ref, gate_ref, sm_ref, cst_ref, s0_ref, cw_ref, alog_ref, dtb_ref, ng_ref,
               yd_ref, cnew_ref, sout_ref, xpad_s, q_s, k_s, v_s, g_s, beta_s, o_s, S_s):
    W = GROUP_W
    hm = _head_masks()
    bd_ones = ((_iota((W, W), 0) >> 6) == (_iota((W, W), 1) >> 6)).astype(bf16)
    exp_g = (_iota((SMALL_W, W), 0) == (_iota((SMALL_W, W), 1) >> 6) + N_HEADS).astype(bf16)
    exp_b = (_iota((SMALL_W, W), 0) == (_iota((SMALL_W, W), 1) >> 6) + 2 * N_HEADS).astype(bf16)

    def head_sum(x):
        hi, lo = _split2(x)
        return _dot(hi, bd_ones) + _dot(lo, bd_ones)

    xpad_s[pl.ds(0, 8), :] = jnp.zeros((8, 3 * W), f32)
    xpad_s[pl.ds(5, 3), :] = cst_ref[0]
    xpad_s[pl.ds(8, L), :] = qkv_ref[0]
    cnew_ref[0] = xpad_s[pl.ds(L + 5, 3), :]
    cw = cw_ref[...]
    neg_a = -jnp.exp(alog_ref[...])
    for t in range(L // R1):
        r0 = t * R1
        rows = pl.ds(r0, R1)
        xc = cw[0:1, :] * xpad_s[pl.ds(5 + r0, R1), :]
        for i in range(1, 4):
            xc = xc + cw[i:i + 1, :] * xpad_s[pl.ds(5 + r0 + i, R1), :]
        xc = xc * jax.nn.sigmoid(xc)
        q = xc[:, 0:W]
        k = xc[:, W:2 * W]
        q_s[rows, :] = q * lax.rsqrt(head_sum(q * q) + RMS_EPS) * (HEAD_DIM ** -0.5)
        k_s[rows, :] = k * lax.rsqrt(head_sum(k * k) + RMS_EPS)
        v_s[rows, :] = xc[:, 2 * W:3 * W]
        sm = sm_ref[0, rows, :]
        g_s[rows, :] = _dot_sel_rhs(neg_a * _softplus(sm + dtb_ref[...]), exp_g)
        beta_s[rows, :] = _dot_sel_rhs(jax.nn.sigmoid(sm), exp_b)

    S_s[...] = s0_ref[0]
    bd_mask = (_iota((W, W), 0) >> 6) == (_iota((W, W), 1) >> 6)

    def chunk(off, C):
        rows = pl.ds(off, C)
        q = q_s[rows, :]
        k = k_s[rows, :]
        v = v_s[rows, :]
        beta = beta_s[rows, :]
        ri = _iota((C, C), 0)
        ci = _iota((C, C), 1)
        ltri = (ri >= ci).astype(bf16)
        gc = _dot_sel_lhs(ltri, g_s[rows, :])
        eg = jnp.exp(gc)
        g_last = gc[C - 1:C, :]
        kb = k * beta
        vb = v * beta
        kbe = kb * eg
        kbf = k.astype(bf16)
        gparts = _split3(gc)
        eye = (ri == ci).astype(f32)
        u = jnp.zeros((C, W), f32)
        w = jnp.zeros((C, W), f32)
        a_in = []
        n_sq = (C - 1).bit_length() - 1
        for h in range(N_HEADS):
            lane_h = (_iota((C, W), 1) == h * HEAD_DIM).astype(bf16)
            g_row = _dot(lane_h, gparts[0], NT) + _dot(lane_h, gparts[1], NT) + _dot(lane_h, gparts[2], NT)
            decay = jnp.exp(jnp.minimum(gc[:, h * HEAD_DIM:h * HEAD_DIM + 1] - g_row, 0.0))
            kbh = jnp.where(hm[h], kb, 0.0).astype(bf16)
            qh = jnp.where(hm[h], q, 0.0).astype(bf16)
            m = jnp.where(ri > ci, _dot(kbh, kbf, NT) * decay, 0.0)
            a_in.append(jnp.where(ri >= ci, _dot(qh, kbf, NT) * decay, 0.0).astype(bf16))
            pw = -m
            tinv = eye + pw
            for _ in range(n_sq):
                pw = _mm3(pw, pw)
                tinv = tinv + _mm3(tinv, pw)
            u = u + _mm3(tinv, jnp.where(hm[h], vb, 0.0))
            w = w + _mm3(tinv, jnp.where(hm[h], kbe, 0.0))
        S = S_s[...]
        Sb = S.astype(bf16)
        v_new = u - _dot(w.astype(bf16), Sb)
        vnb = v_new.astype(bf16)
        o = _dot((q * eg).astype(bf16), Sb)
        for h in range(N_HEADS):
            o = o + jnp.where(hm[h], _dot(a_in[h], vnb), 0.0)
        kd = (k * jnp.exp(g_last - gc)).astype(bf16)
        S_s[...] = S * jnp.exp(g_last) + jnp.where(bd_mask, _dot(kd, vnb, TN), 0.0)
        o_s[rows, :] = o

    C0 = L % DN_CHUNK
    if C0:
        chunk(0, C0)

    def chunk_body(c, carry):
        chunk(pl.multiple_of(C0 + c * DN_CHUNK, 16), DN_CHUNK)
        return carry

    lax.fori_loop(0, L // DN_CHUNK, chunk_body, 0)
    sout_ref[0] = S_s[...]

    for t in range(L // R1):
        rows = pl.ds(t * R1, R1)
        o = o_s[rows, :]
        gate = gate_ref[0, rows, :]
        y = o * lax.rsqrt(head_sum(o * o) * (1.0 / HEAD_DIM) + RMS_EPS) * ng_ref[...]
        yd_ref[0, rows, :] = (y * (gate * jax.nn.sigmoid(gate))).astype(bf16)


def _dn(qkv, gate, small, conv_st, s0_bd, cw, alog_row, dtb_row, ng_row):
    B, L, _ = qkv.shape
    W = GROUP_W
    R1 = _row_tile(L, 768)
    per_b = lambda shape: pl.BlockSpec((1,) + shape, lambda b: (b,) + (0,) * len(shape))
    return pl.pallas_call(
        functools.partial(_dn_kernel, L, R1),
        grid=(B,),
        in_specs=[per_b((L, 3 * W)), per_b((L, W)), per_b((L, SMALL_W)), per_b((3, 3 * W)), per_b((W, W)),
                  _const_spec((4, 3 * W)), _const_spec((1, SMALL_W)), _const_spec((1, SMALL_W)),
                  _const_spec((1, W))],
        out_specs=[per_b((L, W)), per_b((3, 3 * W)), per_b((W, W))],
        out_shape=[jax.ShapeDtypeStruct((B, L, W), bf16), jax.ShapeDtypeStruct((B, 3, 3 * W), f32),
                   jax.ShapeDtypeStruct((B, W, W), f32)],
        scratch_shapes=[pltpu.VMEM((L + 8, 3 * W), f32)] + [pltpu.VMEM((L, W), f32)] * 6
                       + [pltpu.VMEM((W, W), f32)],
        compiler_params=_params("parallel"),
        name="dn",
    )(qkv, gate, small, conv_st, s0_bd, cw, alog_row, dtb_row, ng_row)


def _merge_kernel(alpha, x_ref, ya_ref, yb_ref, yc_ref, yd_ref, wo_ref, g_ref, b_ref, o_ref):
    W = GROUP_W
    m = _dot(ya_ref[...], wo_ref[0:W, :])
    m = m + _dot(yb_ref[...], wo_ref[W:2 * W, :])
    m = m + _dot(yc_ref[...], wo_ref[2 * W:3 * W, :])
    m = m + _dot(yd_ref[...], wo_ref[3 * W:4 * W, :])
    o_ref[...] = _layer_norm(alpha * x_ref[...] + m, g_ref[...], b_ref[...])


def _merge(alpha, x2d, ya, yb, yc, yd, wo, g, b):
    n = x2d.shape[0]
    tm = _row_tile(n, 768)
    row = lambda wd: pl.BlockSpec((tm, wd), lambda i: (i, 0))
    return pl.pallas_call(
        functools.partial(_merge_kernel, alpha),
        grid=(n // tm,),
        in_specs=[row(D_MODEL), row(GROUP_W), row(GROUP_W), row(GROUP_W), row(GROUP_W),
                  _const_spec((D_MODEL, D_MODEL)), _const_spec((1, D_MODEL)), _const_spec((1, D_MODEL))],
        out_specs=row(D_MODEL),
        out_shape=jax.ShapeDtypeStruct((n, D_MODEL), f32),
        compiler_params=_params("parallel"),
        name="merge",
    )(x2d, ya, yb, yc, yd, wo, g, b)


def _ffn_kernel(alpha, Bt, Lt, F, n_chunks, x_ref, st_ref, win_ref, cw_ref, cb_ref, wout_ref, g_ref, b_ref,
                y_ref, nst_ref, carry_s):
    R = Bt * Lt
    Fc = F // n_chunks

    @pl.when(pl.program_id(1) == 0)
    def _():
        carry_s[...] = st_ref[...]

    if Bt == 1:
        x = x_ref[0]
    else:
        x = jnp.concatenate([x_ref[s] for s in range(Bt)], axis=0)
    xb = x.astype(bf16)
    row_in = _iota((R, 1), 0)
    if Bt > 1:
        assert Lt & (Lt - 1) == 0
        row_in = row_in & (Lt - 1)
    first = row_in == 0
    second = row_in == 1

    def rows_of(prev_row, c0):
        parts = [jnp.broadcast_to(carry_s[s, prev_row:prev_row + 1, c0:c0 + Fc], (Lt, Fc)) for s in range(Bt)]
        return parts[0] if Bt == 1 else jnp.concatenate(parts, axis=0)

    acc = jnp.zeros((R, D_MODEL), f32)
    for c in range(n_chunks):
        c0 = c * Fc
        gt = _dot(xb, win_ref[:, c0:c0 + Fc])
        up = _dot(xb, win_ref[:, F + c0:F + c0 + Fc])
        p0 = rows_of(0, c0)
        p1 = rows_of(1, c0)
        g1 = jnp.where(first, p1, pltpu.roll(gt, 1, 0))
        g2 = jnp.where(first, p0, jnp.where(second, p1, pltpu.roll(gt, 2, 0)))
        conv = (cw_ref[0:1, c0:c0 + Fc] * g2 + cw_ref[1:2, c0:c0 + Fc] * g1
                + cw_ref[2:3, c0:c0 + Fc] * gt + cb_ref[:, c0:c0 + Fc])
        hid = (jax.nn.gelu(conv) * up).astype(bf16)
        acc = acc + _dot(hid, wout_ref[c0:c0 + Fc, :])
        for s in range(Bt):
            carry_s[s, :, c0:c0 + Fc] = gt[s * Lt + Lt - 2:s * Lt + Lt, :]
    y = _layer_norm(alpha * x + acc, g_ref[...], b_ref[...])
    for s in range(Bt):
        y_ref[s] = y[s * Lt:(s + 1) * Lt, :]
    nst_ref[...] = carry_s[...]


def _ffn(alpha, x3d, st, win, cw, cb, wout, g, b):
    B, L, _ = x3d.shape
    F = wout.shape[0]
    if L >= 256:
        Bt, Lt = 1, _row_tile(L, 512, 8)
    else:
        Bt, Lt = B, L
    n_chunks = 2
    blk = lambda shape: pl.BlockSpec((Bt,) + shape, lambda i, t: (i, t) + (0,) * (len(shape) - 1))
    st_spec = pl.BlockSpec((Bt, 2, F), lambda i, t: (i, 0, 0))
    return pl.pallas_call(
        functools.partial(_ffn_kernel, alpha, Bt, Lt, F, n_chunks),
        grid=(B // Bt, L // Lt),
        in_specs=[blk((Lt, D_MODEL)), st_spec, _const_spec((D_MODEL, 2 * F)), _const_spec((3, F)),
                  _const_spec((1, F)), _const_spec((F, D_MODEL)), _const_spec((1, D_MODEL)),
                  _const_spec((1, D_MODEL))],
        out_specs=[blk((Lt, D_MODEL)), st_spec],
        out_shape=[jax.ShapeDtypeStruct((B, L, D_MODEL), f32), jax.ShapeDtypeStruct((B, 2, F), f32)],
        scratch_shapes=[pltpu.VMEM((Bt, 2, F), f32)],
        compiler_params=_params("parallel", "arbitrary"),
        name="ffn",
    )(x3d, st, win, cw, cb, wout, g, b)


def _prep_layer_weights(l, P):
    W = GROUP_W
    w_in, b_in = P['w_in'][l], P['b_in'][l]
    o_fox = 2 * W
    o_ff = o_fox + 3 * W
    o_sb = o_ff + N_HEADS
    o_dn = o_sb + 3 * W
    o_da = o_dn + 3 * W
    o_db = o_da + N_HEADS
    o_dg = o_db + N_HEADS
    pad = SMALL_W - 3 * N_HEADS

    def reorder(a):
        parts = [a[..., 0:o_ff], a[..., o_sb:o_da], a[..., o_dg:o_dg + W], a[..., o_ff:o_ff + N_HEADS],
                 a[..., o_da:o_da + 2 * N_HEADS], jnp.zeros(a.shape[:-1] + (pad,), a.dtype)]
        return jnp.concatenate(parts, axis=-1)

    w_r = reorder(w_in).astype(bf16)
    b_r = reorder(b_in).reshape(1, -1)

    def block_diag(w4):
        eye = jnp.eye(N_HEADS, dtype=w4.dtype)
        return (w4[:, :, None, :] * eye[:, None, :, None]).reshape(W, W)

    small_row = lambda vals, off: jnp.zeros((1, SMALL_W), f32).at[0, off:off + N_HEADS].set(vals)
    r1 = lambda a: a.reshape(1, -1)
    return dict(
        w_in=w_r, b_in=b_r,
        lru_cw=P['lru_conv_w'][l], lru_cb=r1(P['lru_conv_b'][l]),
        lru_wa=block_diag(P['lru_w_a'][l]).astype(bf16), lru_wx=block_diag(P['lru_w_x'][l]).astype(bf16),
        lru_ba=r1(P['lru_b_a'][l]), lru_bx=r1(P['lru_b_x'][l]), lru_lam=r1(P['lru_lambda'][l]),
        gains=[r1(P['grp_norm_g'][l][i]) for i in range(3)],
        dn_cw=P['dn_conv_w'][l], dn_alog=small_row(P['dn_a_log'][l], N_HEADS),
        dn_dtb=small_row(P['dn_dt_bias'][l], N_HEADS), dn_ng=r1(jnp.tile(P['dn_norm_g'][l], N_HEADS)),
        w_o=P['w_o'][l].astype(bf16), ln1_g=r1(P['ln1_g'][l]), ln1_b=r1(P['ln1_b'][l]),
        ffn_win=P['ffn_w_in'][l].astype(bf16), ffn_cw=P['ffn_conv_w'][l], ffn_cb=r1(P['ffn_conv_b'][l]),
        ffn_wout=P['ffn_w_out'][l].astype(bf16), ln2_g=r1(P['ln2_g'][l]), ln2_b=r1(P['ln2_b'][l]),
    )


def _embed_block_diag(S):
    B = S.shape[0]
    eye = jnp.eye(N_HEADS, dtype=S.dtype)
    return (S[:, :, :, None, :] * eye[None, :, None, :, None]).reshape(B, GROUP_W, GROUP_W)


def _extract_block_diag(Sbd):
    return jnp.stack([Sbd[:, h * HEAD_DIM:(h + 1) * HEAD_DIM, h * HEAD_DIM:(h + 1) * HEAD_DIM]
                      for h in range(N_HEADS)], axis=1)


def _trunk(x, st, LW, ln_in, alpha):
    B, L, D = x.shape
    W = GROUP_W
    F = LW[0]['ffn_wout'].shape[0]
    new = {n: [] for n in ('lru_conv', 'lru_h', 'fox_k', 'fox_v', 'fox_logf', 'sb_k', 'sb_v',
                           'dn_conv', 'dn_S', 'ffn_conv')}
    x2 = x.reshape(B * L, D)
    for l, w in enumerate(LW):
        outs = _proj(x2, ln_in[0], ln_in[1], w['w_in'], w['b_in'], do_ln=(l == 0))
        if l == 0:
            x2, outs = outs[0], outs[1:]
        lru, fq, fk, fv, sq, sk, sv, dqkv, dgate, small = [o.reshape(B, L, -1) for o in outs]
        if st is None:
            lru_conv = jnp.zeros((B, 3, W), f32)
            lru_h = jnp.zeros((B, 1, W), f32)
            dn_conv = jnp.zeros((B, 3, 3 * W), f32)
            dn_S = jnp.zeros((B, W, W), f32)
            ffn_conv = jnp.zeros((B, 2, F), f32)
            pfk = pfv = plf = psk = psv = None
        else:
            Pn = st['fox_k'].shape[2]
            lru_conv = st['lru_conv'][l]
            lru_h = st['lru_h'][l].reshape(B, 1, W)
            dn_conv = st['dn_conv'][l]
            dn_S = _embed_block_diag(st['dn_S'][l])
            ffn_conv = st['ffn_conv'][l]
            pfk = st['fox_k'][l].reshape(B, Pn, W)
            pfv = st['fox_v'][l].reshape(B, Pn, W)
            plf = jnp.pad(st['fox_logf'][l], ((0, 0), (0, 0), (0, SMALL_W - N_HEADS)))
            psk = st['sb_k'][l].reshape(B, Pn, W)
            psv = st['sb_v'][l].reshape(B, Pn, W)

        ya, lru_conv_new, h_last = _lru(lru, lru_conv, lru_h, w['lru_cw'], w['lru_cb'], w['lru_wa'],
                                        w['lru_wx'], w['lru_ba'], w['lru_bx'], w['lru_lam'], w['gains'][0])
        yb, logf = _fox(fq, fk, fv, small, pfk, pfv, plf, w['gains'][1])
        yc = _sb(sq, sk, sv, psk, psv, w['gains'][2])
        yd, dn_conv_new, S_new = _dn(dqkv, dgate, small, dn_conv, dn_S, w['dn_cw'], w['dn_alog'],
                                     w['dn_dtb'], w['dn_ng'])
        x1 = _merge(alpha, x2, ya.reshape(B * L, W), yb.reshape(B * L, W), yc.reshape(B * L, W),
                    yd.reshape(B * L, W), w['w_o'], w['ln1_g'], w['ln1_b'])
        x3, ffn_new = _ffn(alpha, x1.reshape(B, L, D), ffn_conv, w['ffn_win'], w['ffn_cw'], w['ffn_cb'],
                           w['ffn_wout'], w['ln2_g'], w['ln2_b'])
        x2 = x3.reshape(B * L, D)

        heads = lambda a: a.reshape(B, L, N_HEADS, HEAD_DIM)
        new['lru_conv'].append(lru_conv_new)
        new['lru_h'].append(h_last.reshape(B, W))
        new['fox_k'].append(heads(fk))
        new['fox_v'].append(heads(fv))
        new['fox_logf'].append(logf[:, :, :N_HEADS])
        new['sb_k'].append(heads(sk))
        new['sb_v'].append(heads(sv))
        new['dn_conv'].append(dn_conv_new)
        new['dn_S'].append(_extract_block_diag(S_new))
        new['ffn_conv'].append(ffn_new)
    return x2.reshape(B, L, D), {n: jnp.stack(v) for n, v in new.items()}


def kernel(x_prompt, x_sample, state_lru_conv, state_lru_h, cache_fox_k, cache_fox_v, cache_fox_logf,
           cache_sb_k, cache_sb_v, state_dn_conv, state_dn_S, state_ffn_conv, meta_tokens, ln_in_g, ln_in_b,
           w_in, b_in, lru_conv_w, lru_conv_b, lru_w_a, lru_b_a, lru_w_x, lru_b_x, lru_lambda, dn_conv_w,
           dn_a_log, dn_dt_bias, dn_norm_g, grp_norm_g, w_o, ln1_g, ln1_b, ffn_w_in, ffn_conv_w, ffn_conv_b,
           ffn_w_out, ln2_g, ln2_b):
    P = dict(w_in=w_in, b_in=b_in, lru_conv_w=lru_conv_w, lru_conv_b=lru_conv_b, lru_w_a=lru_w_a,
             lru_b_a=lru_b_a, lru_w_x=lru_w_x, lru_b_x=lru_b_x, lru_lambda=lru_lambda, dn_conv_w=dn_conv_w,
             dn_a_log=dn_a_log, dn_dt_bias=dn_dt_bias, dn_norm_g=dn_norm_g, grp_norm_g=grp_norm_g, w_o=w_o,
             ln1_g=ln1_g, ln1_b=ln1_b, ffn_w_in=ffn_w_in, ffn_conv_w=ffn_conv_w, ffn_conv_b=ffn_conv_b,
             ffn_w_out=ffn_w_out, ln2_g=ln2_g, ln2_b=ln2_b)
    depth = w_in.shape[0]
    assert x_prompt.shape[2] == D_MODEL
    alpha = (2.0 * depth) ** 0.25
    LW = [_prep_layer_weights(l, P) for l in range(depth)]
    ln_in = (ln_in_g.reshape(1, -1), ln_in_b.reshape(1, -1))

    bp = x_prompt.shape[0]
    meta = jnp.broadcast_to(meta_tokens.astype(x_prompt.dtype), (bp, N_META, D_MODEL))
    xp = jnp.concatenate([meta, x_prompt], axis=1)
    yp, ps = _trunk(xp, None, LW, ln_in, alpha)

    st_in = dict(lru_conv=state_lru_conv, lru_h=state_lru_h, fox_k=cache_fox_k, fox_v=cache_fox_v,
                 fox_logf=cache_fox_logf, sb_k=cache_sb_k, sb_v=cache_sb_v, dn_conv=state_dn_conv,
                 dn_S=state_dn_S, ffn_conv=state_ffn_conv)
    ys, ss = _trunk(x_sample, st_in, LW, ln_in, alpha)

    names = ('lru_conv', 'lru_h', 'fox_k', 'fox_v', 'fox_logf', 'sb_k', 'sb_v', 'dn_conv', 'dn_S', 'ffn_conv')
    return (yp[:, N_META:], ys) + tuple(ps[n] for n in names) + tuple(ss[n] for n in names)
```

```python
import functools

import jax
import jax.numpy as jnp
from jax import lax
from jax.experimental import pallas as pl
from jax.experimental.pallas import tpu as pltpu

f32 = jnp.float32
bf16 = jnp.bfloat16

D_MODEL = 1024
GROUP_W = 256
N_HEADS = 4
HEAD_DIM = 64
N_META = 16
LRU_C = 8.0
LN_EPS = 1e-5
RMS_EPS = 1e-6
ATT_TILE = 256
ATT_SUB = 128
DN_CHUNK = 64
NEG = -1e30
LOG2E = 1.4426950408889634
SMALL_W = 128
VMEM_LIMIT_BYTES = 56 * 1024 * 1024

NN = (((1,), (0,)), ((), ()))
NT = (((1,), (1,)), ((), ()))
TN = (((0,), (0,)), ((), ()))


def _dot(a, b, dims=NN):
    return lax.dot_general(a, b, dims, preferred_element_type=f32)


def _split2(x):
    hi = x.astype(bf16)
    lo = (x - hi.astype(f32)).astype(bf16)
    return hi, lo


def _split3(x):
    p0 = x.astype(bf16)
    r = x - p0.astype(f32)
    p1 = r.astype(bf16)
    p2 = (r - p1.astype(f32)).astype(bf16)
    return p0, p1, p2


def _dot_sel_rhs(x, sel, dims=NN):
    p0, p1, p2 = _split3(x)
    return _dot(p0, sel, dims) + _dot(p1, sel, dims) + _dot(p2, sel, dims)


def _dot_sel_lhs(sel, x, dims=NN):
    p0, p1, p2 = _split3(x)
    return _dot(sel, p0, dims) + _dot(sel, p1, dims) + _dot(sel, p2, dims)


def _iota(shape, dim):
    return lax.broadcasted_iota(jnp.int32, shape, dim)


def _head_masks(width=GROUP_W):
    lane_head = _iota((1, width), 1) >> 6
    return [lane_head == h for h in range(N_HEADS)]


def _softplus(x):
    return jnp.maximum(x, 0.0) + jnp.log1p(jnp.exp(-jnp.abs(x)))


def _log_sigmoid(x):
    return jnp.minimum(x, 0.0) - jnp.log1p(jnp.exp(-jnp.abs(x)))


def _layer_norm(x, g, b):
    mu = jnp.mean(x, -1, keepdims=True)
    xc = x - mu
    var = jnp.mean(xc * xc, -1, keepdims=True)
    return xc * lax.rsqrt(var + LN_EPS) * g + b


def _rms_norm(x, g):
    return x * lax.rsqrt(jnp.mean(x * x, -1, keepdims=True) + RMS_EPS) * g


def _row_tile(n, cap, mult=16):
    best = None
    for d in range(mult, min(n, cap) + 1, mult):
        if n % d == 0:
            best = d
    return best if best is not None else n


def _const_spec(shape):
    nd = len(shape)
    return pl.BlockSpec(shape, lambda *_: (0,) * nd, pipeline_mode=pl.Buffered(1))


def _params(*sem):
    return pltpu.CompilerParams(dimension_semantics=sem, vmem_limit_bytes=VMEM_LIMIT_BYTES)


PROJ_WIDTHS = (512, 256, 256, 256, 256, 256, 256, 768, 256, SMALL_W)


def _proj_kernel(do_ln, x_ref, g_ref, b_ref, w_ref, bias_ref, *outs):
    x = x_ref[...]
    if do_ln:
        x = _layer_norm(x, g_ref[...], b_ref[...])
        outs[0][...] = x
        outs = outs[1:]
    xb = x.astype(bf16)
    col = 0
    for o, wd in zip(outs, PROJ_WIDTHS):
        o[...] = _dot(xb, w_ref[:, col:col + wd]) + bias_ref[:, col:col + wd]
        col += wd


def _proj(x2d, ln_g, ln_b, w, bias, do_ln):
    n = x2d.shape[0]
    tm = _row_tile(n, 768)
    wtot = w.shape[1]
    row = lambda wd: pl.BlockSpec((tm, wd), lambda i: (i, 0))
    out_shape = [jax.ShapeDtypeStruct((n, wd), f32) for wd in PROJ_WIDTHS]
    out_specs = [row(wd) for wd in PROJ_WIDTHS]
    if do_ln:
        out_shape = [jax.ShapeDtypeStruct((n, D_MODEL), f32)] + out_shape
        out_specs = [row(D_MODEL)] + out_specs
    return pl.pallas_call(
        functools.partial(_proj_kernel, do_ln),
        grid=(n // tm,),
        in_specs=[row(D_MODEL), _const_spec((1, D_MODEL)), _const_spec((1, D_MODEL)),
                  _const_spec((D_MODEL, wtot)), _const_spec((1, wtot))],
        out_specs=out_specs,
        out_shape=out_shape,
        compiler_params=_params("parallel"),
        name="proj",
    )(x2d, ln_g, ln_b, w, bias)


def _lru_kernel(L, R1, lru_ref, cst_ref, h0_ref, cw_ref, cb_ref, wa_ref, wx_ref, ba_ref, bx_ref,
                lam_ref, gain_ref, ya_ref, cnew_ref, hlast_ref, xpad_s, a_s, b_s):
    W = GROUP_W
    xpad_s[pl.ds(0, 8), :] = jnp.zeros((8, W), f32)
    xpad_s[pl.ds(5, 3), :] = cst_ref[0]
    xpad_s[pl.ds(8, L), :] = lru_ref[0, :, 0:W]
    cnew_ref[0] = xpad_s[pl.ds(L + 5, 3), :]

    sp = _softplus(-lam_ref[...])
    cw = cw_ref[...]
    for t in range(L // R1):
        r0 = t * R1
        xa = cb_ref[...]
        for i in range(4):
            xa = xa + cw[i:i + 1, :] * xpad_s[pl.ds(5 + r0 + i, R1), :]
        xb = xa.astype(bf16)
        r = jax.nn.sigmoid(_dot(xb, wa_ref[...]) + ba_ref[...])
        ig = jax.nn.sigmoid(_dot(xb, wx_ref[...]) + bx_ref[...])
        log_a = (-LRU_C) * r * sp
        y2 = 2.0 * log_a
        one_m_a2 = jnp.tanh(-0.5 * y2) * (1.0 + jnp.exp(y2))
        a_s[pl.ds(r0, R1), :] = jnp.exp(log_a)
        b_s[pl.ds(r0, R1), :] = jnp.sqrt(one_m_a2) * (ig * xa)

    rows8 = _iota((8, W), 0)

    def scan_body(g, hprev):
        off = pl.multiple_of(g * 8, 8)
        A = a_s[pl.ds(off, 8), :]
        Bv = b_s[pl.ds(off, 8), :]
        for s in (1, 2, 4):
            keep = rows8 >= s
            a_sh = jnp.where(keep, pltpu.roll(A, s, 0), 1.0)
            b_sh = jnp.where(keep, pltpu.roll(Bv, s, 0), 0.0)
            Bv = A * b_sh + Bv
            A = A * a_sh
        h = A * hprev + Bv
        b_s[pl.ds(off, 8), :] = h
        return h[7:8, :]

    hlast_ref[0] = lax.fori_loop(0, L // 8, scan_body, h0_ref[0])

    for t in range(L // R1):
        r0 = t * R1
        o = jax.nn.gelu(lru_ref[0, pl.ds(r0, R1), W:2 * W]) * b_s[pl.ds(r0, R1), :]
        ya_ref[0, pl.ds(r0, R1), :] = _rms_norm(o, gain_ref[...]).astype(bf16)


def _lru(lru3d, conv_st, h0, cw, cb, wa, wx, ba, bx, lam, gain):
    B, L, _ = lru3d.shape
    R1 = _row_tile(L, 768)
    W = GROUP_W
    per_b = lambda shape: pl.BlockSpec((1,) + shape, lambda b: (b,) + (0,) * len(shape))
    return pl.pallas_call(
        functools.partial(_lru_kernel, L, R1),
        grid=(B,),
        in_specs=[per_b((L, 2 * W)), per_b((3, W)), per_b((1, W)),
                  _const_spec((4, W)), _const_spec((1, W)), _const_spec((W, W)), _const_spec((W, W)),
                  _const_spec((1, W)), _const_spec((1, W)), _const_spec((1, W)), _const_spec((1, W))],
        out_specs=[per_b((L, W)), per_b((3, W)), per_b((1, W))],
        out_shape=[jax.ShapeDtypeStruct((B, L, W), bf16), jax.ShapeDtypeStruct((B, 3, W), f32),
                   jax.ShapeDtypeStruct((B, 1, W), f32)],
        scratch_shapes=[pltpu.VMEM((L + 8, W), f32), pltpu.VMEM((L, W), f32), pltpu.VMEM((L, W), f32)],
        compiler_params=_params("parallel"),
        name="lru",
    )(lru3d, conv_st, h0, cw, cb, wa, wx, ba, bx, lam, gain)


def _att_layout(L, P):
    T = ATT_TILE
    assert (P + L) % 16 == 0
    front = (-(P + L)) % T
    Lpad = front + P + L
    L0 = L % T
    nT = L // T
    base_t = (front + P + L0) // T
    return front, Lpad, L0, nT, base_t


def _stage_pairs(front, P, L, new_ref, past_ref, flat_s, pair_s, hm):
    W = GROUP_W
    S = ATT_SUB
    if front:
        flat_s[pl.ds(0, front), :] = jnp.zeros((front, W), bf16)
    if P:
        flat_s[pl.ds(front, P), :] = past_ref[0].astype(bf16)
    flat_s[pl.ds(front + P, L), :] = new_ref[0].astype(bf16)
    zero = jnp.zeros((S, W), bf16)

    def body(u, c):
        x = flat_s[pl.ds(pl.multiple_of(u * S, S), S), :]
        base = pl.multiple_of(u * 2 * S, 2 * S)
        for p in range(2):
            pair_s[p, pl.ds(base, S), :] = jnp.where(hm[2 * p], x, zero)
            pair_s[p, pl.ds(base + S, S), :] = jnp.where(hm[2 * p + 1], x, zero)
        return c

    lax.fori_loop(0, (front + P + L) // S, body, 0)


def _pair_scores(qb, kpair_s, j):
    S2 = 2 * ATT_SUB
    kbase = j * (2 * ATT_TILE)
    return [[_dot(qb, kpair_s[p, pl.ds(pl.multiple_of(kbase + sub * S2, S2), S2), :], NT)
             for sub in range(2)] for p in range(2)]


def _pair_pv(wb, vpair_s, j):
    S2 = 2 * ATT_SUB
    kbase = j * (2 * ATT_TILE)
    out = None
    for sub in range(2):
        for p in range(2):
            lhs = jnp.concatenate([wb[2 * p][sub], wb[2 * p + 1][sub]], axis=1)
            t = _dot(lhs, vpair_s[p, pl.ds(pl.multiple_of(kbase + sub * S2, S2), S2), :])
            out = t if out is None else out + t
    return out


def _by_head_lanes(vals, lo_half):
    return jnp.concatenate([jnp.where(lo_half, vals[0], vals[1]), jnp.where(lo_half, vals[2], vals[3])], axis=1)


def _fox_kernel(L, P, *refs):
    T, S = ATT_TILE, ATT_SUB
    front, Lpad, L0, nT, base_t = _att_layout(L, P)
    if P:
        q_ref, k_ref, v_ref, sm_ref, pk_ref, pv_ref, plf_ref, gain_ref, yb_ref, lf_ref = refs[:10]
        scr = refs[10:]
    else:
        q_ref, k_ref, v_ref, sm_ref, gain_ref, yb_ref, lf_ref = refs[:7]
        pk_ref = pv_ref = plf_ref = None
        scr = refs[7:]
    flat_s, kpair_s, vpair_s, lfp_s, F_s, FT_s, qb_s, fqb_s, m_s, lp_s, acc_s = scr
    hm = _head_masks()
    lo_half = _iota((1, S), 1) < HEAD_DIM
    scale = HEAD_DIM ** -0.5 * LOG2E

    _stage_pairs(front, P, L, k_ref, pk_ref, flat_s, kpair_s, hm)
    _stage_pairs(front, P, L, v_ref, pv_ref, flat_s, vpair_s, hm)

    if front:
        lfp_s[pl.ds(0, front), :] = jnp.zeros((front, SMALL_W), f32)
    if P:
        lfp_s[pl.ds(front, P), :] = plf_ref[0]
    lf = _log_sigmoid(sm_ref[0])
    lf_ref[0] = lf
    lfp_s[pl.ds(front + P, L), :] = lf
    ltri = (_iota((S, S), 0) >= _iota((S, S), 1)).astype(bf16)
    sel8 = (_iota((8, SMALL_W), 0) == _iota((8, SMALL_W), 1)).astype(bf16)

    def cum_body(u, carry):
        r = pl.multiple_of(u * S, S)
        F = _dot_sel_lhs(ltri, lfp_s[pl.ds(r, S), :]) + carry
        F2 = F * LOG2E
        F_s[pl.ds(r, S), :] = F2
        FT_s[:, pl.ds(r, S)] = _dot_sel_lhs(sel8, F2, NT)
        return F[S - 1:S, :]

    lax.fori_loop(0, Lpad // S, cum_body, jnp.zeros((1, SMALL_W), f32))

    def process(qnat, tq, d):
        rows = pl.ds(0, tq)
        qp0 = front + P + qnat
        qb_s[rows, :] = (q_ref[0, pl.ds(qnat, tq), :] * scale).astype(bf16)
        Fq = F_s[pl.ds(qp0, tq), :]
        for h in range(N_HEADS):
            fqb_s[h, rows, :] = jnp.broadcast_to(Fq[:, h:h + 1], (tq, S))
            m_s[h, rows, :] = jnp.full((tq, S), NEG, f32)
            lp_s[h, rows, :] = jnp.zeros((tq, S), f32)
        acc_s[rows, :] = jnp.zeros((tq, GROUP_W), f32)

        def scores(j):
            return _pair_scores(qb_s[rows, :], kpair_s, j)

        def update(j, sc, masked):
            if masked:
                qpos = qp0 + _iota((tq, S), 0)
                vis = []
                for sub in range(2):
                    kpos = j * T + sub * S + _iota((tq, S), 1)
                    ok = kpos <= qpos
                    if front:
                        ok = ok & (kpos >= front)
                    vis.append(ok)
            alphas, pb = [], []
            for h in range(N_HEADS):
                p_, half = divmod(h, 2)
                fq = fqb_s[h, rows, :]
                c = []
                for sub in range(2):
                    fk = FT_s[h:h + 1, pl.ds(pl.multiple_of(j * T + sub * S, S), S)]
                    x = sc[p_][sub][:, half * S:(half + 1) * S] + (fq - fk)
                    if masked:
                        x = jnp.where(vis[sub], x, NEG)
                    c.append(x)
                m_old = m_s[h, rows, :]
                m_new = jnp.maximum(m_old, jnp.max(jnp.maximum(c[0], c[1]), -1, keepdims=True))
                alpha = jnp.exp2(m_old - m_new)
                e0 = jnp.exp2(c[0] - m_new)
                e1 = jnp.exp2(c[1] - m_new)
                lp_s[h, rows, :] = alpha * lp_s[h, rows, :] + (e0 + e1)
                m_s[h, rows, :] = m_new
                alphas.append(alpha)
                pb.append((e0.astype(bf16), e1.astype(bf16)))
            pv = _pair_pv(pb, vpair_s, j)
            acc_s[rows, :] = acc_s[rows, :] * _by_head_lanes(alphas, lo_half) + pv

        def kv_step(j, masked):
            update(j, scores(j), masked)

        def plain2(t, c):
            j = jf + 2 * t
            sc0 = scores(j)
            sc1 = scores(j + 1)
            update(j, sc0, False)
            update(j + 1, sc1, False)
            return c

        jf = 1 if front else 0
        if (not isinstance(d, int)) or d > 0:
            if front:
                kv_step(0, True)
            n_plain = d - jf
            lax.fori_loop(0, n_plain // 2, plain2, 0)

            @pl.when(n_plain % 2 == 1)
            def _():
                kv_step(d - 1, False)
        kv_step(d, True)

        l = [jnp.sum(lp_s[h, rows, :], -1, keepdims=True) for h in range(N_HEADS)]
        out = acc_s[rows, :] / _by_head_lanes(l, lo_half)
        yb_ref[0, pl.ds(qnat, tq), :] = _rms_norm(out, gain_ref[...]).astype(bf16)

    if L0:
        process(0, L0, base_t - 1)

    def q_body(i, c):
        process(pl.multiple_of(L0 + i * T, 16), T, base_t + i)
        return c

    lax.fori_loop(0, nT, q_body, 0)


def _fox(q, k, v, small, past_k, past_v, past_lf, gain):
    B, L, W = q.shape
    P = 0 if past_k is None else past_k.shape[1]
    T = ATT_TILE
    _, Lpad, _, _, _ = _att_layout(L, P)
    per_b = lambda shape: pl.BlockSpec((1,) + shape, lambda b: (b,) + (0,) * len(shape))
    in_specs = [per_b((L, W)), per_b((L, W)), per_b((L, W)), per_b((L, SMALL_W))]
    args = [q, k, v, small]
    if P:
        in_specs += [per_b((P, W)), per_b((P, W)), per_b((P, SMALL_W))]
        args += [past_k, past_v, past_lf]
    in_specs.append(_const_spec((1, W)))
    args.append(gain)
    return pl.pallas_call(
        functools.partial(_fox_kernel, L, P),
        grid=(B,),
        in_specs=in_specs,
        out_specs=[per_b((L, W)), per_b((L, SMALL_W))],
        out_shape=[jax.ShapeDtypeStruct((B, L, W), bf16), jax.ShapeDtypeStruct((B, L, SMALL_W), f32)],
        scratch_shapes=[pltpu.VMEM((Lpad, W), bf16), pltpu.VMEM((2, 2 * Lpad, W), bf16),
                        pltpu.VMEM((2, 2 * Lpad, W), bf16), pltpu.VMEM((Lpad, SMALL_W), f32),
                        pltpu.VMEM((Lpad, SMALL_W), f32), pltpu.VMEM((8, Lpad), f32),
                        pltpu.VMEM((T, W), bf16), pltpu.VMEM((N_HEADS, T, ATT_SUB), f32),
                        pltpu.VMEM((N_HEADS, T, ATT_SUB), f32), pltpu.VMEM((N_HEADS, T, ATT_SUB), f32),
                        pltpu.VMEM((T, W), f32)],
        compiler_params=_params("parallel"),
        name="fox",
    )(*args)


def _sb_kernel(L, P, *refs):
    T, S = ATT_TILE, ATT_SUB
    front, Lpad, L0, nT, base_t = _att_layout(L, P)
    if P:
        q_ref, k_ref, v_ref, pk_ref, pv_ref, gain_ref, yc_ref = refs[:7]
        scr = refs[7:]
    else:
        q_ref, k_ref, v_ref, gain_ref, yc_ref = refs[:5]
        pk_ref = pv_ref = None
        scr = refs[5:]
    flat_s, kpair_s, vpair_s, qb_s, c_s, acc_s = scr
    hm = _head_masks()
    scale = HEAD_DIM ** -0.5 * LOG2E

    _stage_pairs(front, P, L, k_ref, pk_ref, flat_s, kpair_s, hm)
    _stage_pairs(front, P, L, v_ref, pv_ref, flat_s, vpair_s, hm)

    after2 = ((_iota((2 * T, T), 0) & (T - 1)) > _iota((2 * T, T), 1)).astype(bf16)

    def process(qnat, tq, d):
        rows = pl.ds(0, tq)
        qp0 = front + P + qnat
        qb_s[rows, :] = (q_ref[0, pl.ds(qnat, tq), :] * scale).astype(bf16)
        for h in range(N_HEADS):
            c_s[h, rows, :] = jnp.zeros((tq, S), f32)
        acc_s[rows, :] = jnp.zeros((tq, GROUP_W), f32)

        def scores(j):
            return _pair_scores(qb_s[rows, :], kpair_s, j)

        def update(j, sc, masked):
            if masked:
                kpos = j * T + _iota((tq, T), 1)
                vis = kpos < qp0 + _iota((tq, T), 0)
                if front:
                    vis = vis & (kpos >= front)
            wb = []
            for h in range(N_HEADS):
                p_, half = divmod(h, 2)
                z = jnp.concatenate([sc[p_][sub][:, half * S:(half + 1) * S] for sub in range(2)], axis=1)
                ls = jnp.minimum(z, 0.0) - jnp.log2(1.0 + jnp.exp2(-jnp.abs(z)))
                lk = ls - z
                if masked:
                    lk = jnp.where(vis, lk, 0.0)
                hi, lo = _split2(lk)
                later = _dot(jnp.concatenate([hi, lo], axis=1), after2)
                c = c_s[h, rows, :]
                w = jnp.exp2(ls + later + jnp.concatenate([c, c], axis=1))
                if masked:
                    w = jnp.where(vis, w, 0.0)
                c_s[h, rows, :] = c + jnp.sum(lk, -1, keepdims=True)
                wbf = w.astype(bf16)
                wb.append((wbf[:, 0:S], wbf[:, S:2 * S]))
            acc_s[rows, :] = acc_s[rows, :] + _pair_pv(wb, vpair_s, j)

        def kv_step(j, masked):
            update(j, scores(j), masked)

        def plain2(t, c):
            j = d - 1 - 2 * t
            sc0 = scores(j)
            sc1 = scores(j - 1)
            update(j, sc0, False)
            update(j - 1, sc1, False)
            return c

        jf = 1 if front else 0
        kv_step(d, True)
        if (not isinstance(d, int)) or d > 0:
            n_plain = d - jf
            lax.fori_loop(0, n_plain // 2, plain2, 0)

            @pl.when(n_plain % 2 == 1)
            def _():
                kv_step(jf, False)
            if front:
                kv_step(0, True)

        yc_ref[0, pl.ds(qnat, tq), :] = _rms_norm(acc_s[rows, :], gain_ref[...]).astype(bf16)

    if L0:
        process(0, L0, base_t - 1)

    def q_body(i, c):
        process(pl.multiple_of(L0 + i * T, 16), T, base_t + i)
        return c

    lax.fori_loop(0, nT, q_body, 0)


def _sb(q, k, v, past_k, past_v, gain):
    B, L, W = q.shape
    P = 0 if past_k is None else past_k.shape[1]
    T = ATT_TILE
    _, Lpad, _, _, _ = _att_layout(L, P)
    per_b = lambda shape: pl.BlockSpec((1,) + shape, lambda b: (b,) + (0,) * len(shape))
    in_specs = [per_b((L, W)), per_b((L, W)), per_b((L, W))]
    args = [q, k, v]
    if P:
        in_specs += [per_b((P, W)), per_b((P, W))]
        args += [past_k, past_v]
    in_specs.append(_const_spec((1, W)))
    args.append(gain)
    return pl.pallas_call(
        functools.partial(_sb_kernel, L, P),
        grid=(B,),
        in_specs=in_specs,
        out_specs=per_b((L, W)),
        out_shape=jax.ShapeDtypeStruct((B, L, W), bf16),
        scratch_shapes=[pltpu.VMEM((Lpad, W), bf16), pltpu.VMEM((2, 2 * Lpad, W), bf16),
                        pltpu.VMEM((2, 2 * Lpad, W), bf16), pltpu.VMEM((T, W), bf16),
                        pltpu.VMEM((N_HEADS, T, ATT_SUB), f32), pltpu.VMEM((T, W), f32)],
        compiler_params=_params("parallel"),
        name="sb",
    )(*args)


def _dn_kernel(L, R1, qkv_ref, gate_ref, sm_ref, cst_ref, s0_ref, cw_ref, alog_ref, dtb_ref, ng_ref,
               yd_ref, cnew_ref, sout_ref, xpad_s, q_s, k_s, v_s, g_s, beta_s, o_s, S_s):
    W = GROUP_W
    hm = _head_masks()
    bd_ones = ((_iota((W, W), 0) >> 6) == (_iota((W, W), 1) >> 6)).astype(bf16)
    exp_g = (_iota((SMALL_W, W), 0) == (_iota((SMALL_W, W), 1) >> 6) + N_HEADS).astype(bf16)
    exp_b = (_iota((SMALL_W, W), 0) == (_iota((SMALL_W, W), 1) >> 6) + 2 * N_HEADS).astype(bf16)

    def head_sum(x):
        hi, lo = _split2(x)
        return _dot(hi, bd_ones) + _dot(lo, bd_ones)

    xpad_s[pl.ds(0, 8), :] = jnp.zeros((8, 3 * W), f32)
    xpad_s[pl.ds(5, 3), :] = cst_ref[0]
    xpad_s[pl.ds(8, L), :] = qkv_ref[0]
    cnew_ref[0] = xpad_s[pl.ds(L + 5, 3), :]
    cw = cw_ref[...]
    neg_a = -jnp.exp(alog_ref[...])
    for t in range(L // R1):
        r0 = t * R1
        rows = pl.ds(r0, R1)
        xc = cw[0:1, :] * xpad_s[pl.ds(5 + r0, R1), :]
        for i in range(1, 4):
            xc = xc + cw[i:i + 1, :] * xpad_s[pl.ds(5 + r0 + i, R1), :]
        xc = xc * jax.nn.sigmoid(xc)
        q = xc[:, 0:W]
        k = xc[:, W:2 * W]
        q_s[rows, :] = q * lax.rsqrt(head_sum(q * q) + RMS_EPS) * (HEAD_DIM ** -0.5)
        k_s[rows, :] = k * lax.rsqrt(head_sum(k * k) + RMS_EPS)
        v_s[rows, :] = xc[:, 2 * W:3 * W]
        sm = sm_ref[0, rows, :]
        g_s[rows, :] = _dot_sel_rhs(neg_a * _softplus(sm + dtb_ref[...]), exp_g)
        beta_s[rows, :] = _dot_sel_rhs(jax.nn.sigmoid(sm), exp_b)

    S_s[...] = s0_ref[0]
    bd_mask = (_iota((W, W), 0) >> 6) == (_iota((W, W), 1) >> 6)

    def part_a(offs, C):
        each = lambda f, *ls: [f(*a) for a in zip(*ls)]
        CW = N_HEADS * C
        assert C & (C - 1) == 0
        ri = _iota((C, CW), 0)
        si = _iota((C, CW), 1) & (C - 1)
        blk = [(_iota((1, CW), 1) >> (C.bit_length() - 1)) == h for h in range(N_HEADS)]
        ltri = (_iota((C, C), 0) >= _iota((C, C), 1)).astype(bf16)

        def head_rows(x):
            xb = x.astype(bf16)
            return jnp.concatenate([jnp.where(hm[h], xb, jnp.zeros_like(xb)) for h in range(N_HEADS)], axis=0)

        def head_blocks(y):
            yb = y.astype(bf16)
            return jnp.concatenate([jnp.where(blk[h], yb, jnp.zeros_like(yb)) for h in range(N_HEADS)], axis=0)

        q = [q_s[pl.ds(off, C), :] for off in offs]
        k = [k_s[pl.ds(off, C), :] for off in offs]
        beta = [beta_s[pl.ds(off, C), :] for off in offs]
        gc = [_dot_sel_lhs(ltri, g_s[pl.ds(off, C), :]) for off in offs]
        eg = each(jnp.exp, gc)
        g_last = [g[C - 1:C, :] for g in gc]
        kb = each(lambda a, b: a * b, k, beta)
        vb = [v_s[pl.ds(off, C), :] * b for off, b in zip(offs, beta)]
        kbe = each(lambda a, b: a * b, kb, eg)

        gparts = each(_split3, gc)
        first_lane = ((_iota((C, W), 1) & (HEAD_DIM - 1)) == 0).astype(bf16)
        first3 = jnp.concatenate([first_lane] * 3, axis=1)
        g_row = [_dot(first3, jnp.concatenate([head_rows(p) for p in gp], axis=1), NT) for gp in gparts]
        if CW == W:
            g_col = gc
        else:
            spread = (_iota((W, CW), 0) == (_iota((W, CW), 1) >> (C.bit_length() - 1)) * HEAD_DIM).astype(bf16)
            spread3 = jnp.concatenate([spread] * 3, axis=0)
            g_col = [_dot(jnp.concatenate(gp, axis=1), spread3) for gp in gparts]
        decay = each(lambda c, r: jnp.exp(jnp.minimum(c - r, 0.0)), g_col, g_row)
        k_rows = each(head_rows, k)
        m = each(lambda a, kr, d: jnp.where(ri > si, _dot(a.astype(bf16), kr, NT) * d, 0.0), kb, k_rows, decay)
        a_in = each(lambda a, kr, d: jnp.where(ri >= si, _dot(a.astype(bf16), kr, NT) * d, 0.0).astype(bf16),
                    q, k_rows, decay)
        pw = [-a for a in m]
        x = pw
        pw_bd = each(head_blocks, pw)
        for _ in range((C - 1).bit_length() - 1):
            pw = each(lambda a, b: _dot(a.astype(bf16), b), pw, pw_bd)
            pw_bd = each(head_blocks, pw)
            x = each(lambda a, p, b: a + p + _dot(a.astype(bf16), b), x, pw, pw_bd)
        xb = [a.astype(bf16) for a in x]
        u = each(lambda a, b: b + _dot(a, head_rows(b)), xb, vb)
        w = each(lambda a, b: (b + _dot(a, head_rows(b))).astype(bf16), xb, kbe)
        qe = each(lambda a, b: (a * b).astype(bf16), q, eg)
        kd = each(lambda a, gl, g: (a * jnp.exp(gl - g)).astype(bf16), k, g_last, gc)
        return list(zip(u, w, a_in, qe, kd, each(jnp.exp, g_last)))

    def part_b(off, C, u, wb, a_in, qe, kd, eg_last):
        S = S_s[...]
        Sb = S.astype(bf16)
        v_new = u - _dot(wb, Sb)
        vnb = v_new.astype(bf16)
        v_rows = jnp.concatenate([jnp.where(hm[h], vnb, jnp.zeros_like(vnb)) for h in range(N_HEADS)], axis=0)
        o_s[pl.ds(off, C), :] = _dot(qe, Sb) + _dot(a_in, v_rows)
        S_s[...] = S * eg_last + jnp.where(bd_mask, _dot(kd, vnb, TN), 0.0)

    C0 = L % DN_CHUNK
    if C0:
        part_b(0, C0, *part_a([0], C0)[0])

    n_blocks = L // DN_CHUNK
    group = max(g for g in (8, 4, 2, 1) if n_blocks % g == 0)

    def group_body(c, carry):
        offs = [pl.multiple_of(C0 + (c * group + i) * DN_CHUNK, 16) for i in range(group)]
        for off, a in zip(offs, part_a(offs, DN_CHUNK)):
            part_b(off, DN_CHUNK, *a)
        return carry

    lax.fori_loop(0, n_blocks // group, group_body, 0)
    sout_ref[0] = S_s[...]

    for t in range(L // R1):
        rows = pl.ds(t * R1, R1)
        o = o_s[rows, :]
        gate = gate_ref[0, rows, :]
        y = o * lax.rsqrt(head_sum(o * o) * (1.0 / HEAD_DIM) + RMS_EPS) * ng_ref[...]
        yd_ref[0, rows, :] = (y * (gate * jax.nn.sigmoid(gate))).astype(bf16)


def _dn(qkv, gate, small, conv_st, s0_bd, cw, alog_row, dtb_row, ng_row):
    B, L, _ = qkv.shape
    W = GROUP_W
    R1 = _row_tile(L, 768)
    per_b = lambda shape: pl.BlockSpec((1,) + shape, lambda b: (b,) + (0,) * len(shape))
    return pl.pallas_call(
        functools.partial(_dn_kernel, L, R1),
        grid=(B,),
        in_specs=[per_b((L, 3 * W)), per_b((L, W)), per_b((L, SMALL_W)), per_b((3, 3 * W)), per_b((W, W)),
                  _const_spec((4, 3 * W)), _const_spec((1, SMALL_W)), _const_spec((1, SMALL_W)),
                  _const_spec((1, W))],
        out_specs=[per_b((L, W)), per_b((3, 3 * W)), per_b((W, W))],
        out_shape=[jax.ShapeDtypeStruct((B, L, W), bf16), jax.ShapeDtypeStruct((B, 3, 3 * W), f32),
                   jax.ShapeDtypeStruct((B, W, W), f32)],
        scratch_shapes=[pltpu.VMEM((L + 8, 3 * W), f32)] + [pltpu.VMEM((L, W), f32)] * 6
                       + [pltpu.VMEM((W, W), f32)],
        compiler_params=_params("parallel"),
        name="dn",
    )(qkv, gate, small, conv_st, s0_bd, cw, alog_row, dtb_row, ng_row)


def _merge_kernel(alpha, x_ref, ya_ref, yb_ref, yc_ref, yd_ref, wo_ref, g_ref, b_ref, o_ref):
    W = GROUP_W
    m = _dot(ya_ref[...], wo_ref[0:W, :])
    m = m + _dot(yb_ref[...], wo_ref[W:2 * W, :])
    m = m + _dot(yc_ref[...], wo_ref[2 * W:3 * W, :])
    m = m + _dot(yd_ref[...], wo_ref[3 * W:4 * W, :])
    o_ref[...] = _layer_norm(alpha * x_ref[...] + m, g_ref[...], b_ref[...])


def _merge(alpha, x2d, ya, yb, yc, yd, wo, g, b):
    n = x2d.shape[0]
    tm = _row_tile(n, 768)
    row = lambda wd: pl.BlockSpec((tm, wd), lambda i: (i, 0))
    return pl.pallas_call(
        functools.partial(_merge_kernel, alpha),
        grid=(n // tm,),
        in_specs=[row(D_MODEL), row(GROUP_W), row(GROUP_W), row(GROUP_W), row(GROUP_W),
                  _const_spec((D_MODEL, D_MODEL)), _const_spec((1, D_MODEL)), _const_spec((1, D_MODEL))],
        out_specs=row(D_MODEL),
        out_shape=jax.ShapeDtypeStruct((n, D_MODEL), f32),
        compiler_params=_params("parallel"),
        name="merge",
    )(x2d, ya, yb, yc, yd, wo, g, b)


def _ffn_kernel(alpha, Bt, Lt, F, n_chunks, x_ref, st_ref, win_ref, cw_ref, cb_ref, wout_ref, g_ref, b_ref,
                y_ref, nst_ref, carry_s):
    R = Bt * Lt
    Fc = F // n_chunks

    @pl.when(pl.program_id(1) == 0)
    def _():
        carry_s[...] = st_ref[...]

    if Bt == 1:
        x = x_ref[0]
    else:
        x = jnp.concatenate([x_ref[s] for s in range(Bt)], axis=0)
    xb = x.astype(bf16)
    row_in = _iota((R, 1), 0)
    if Bt > 1:
        assert Lt & (Lt - 1) == 0
        row_in = row_in & (Lt - 1)
    first = row_in == 0
    second = row_in == 1

    def rows_of(prev_row, c0):
        parts = [jnp.broadcast_to(carry_s[s, prev_row:prev_row + 1, c0:c0 + Fc], (Lt, Fc)) for s in range(Bt)]
        return parts[0] if Bt == 1 else jnp.concatenate(parts, axis=0)

    acc = jnp.zeros((R, D_MODEL), f32)
    for c in range(n_chunks):
        c0 = c * Fc
        gt = _dot(xb, win_ref[:, c0:c0 + Fc])
        up = _dot(xb, win_ref[:, F + c0:F + c0 + Fc])
        p0 = rows_of(0, c0)
        p1 = rows_of(1, c0)
        g1 = jnp.where(first, p1, pltpu.roll(gt, 1, 0))
        g2 = jnp.where(first, p0, jnp.where(second, p1, pltpu.roll(gt, 2, 0)))
        conv = (cw_ref[0:1, c0:c0 + Fc] * g2 + cw_ref[1:2, c0:c0 + Fc] * g1
                + cw_ref[2:3, c0:c0 + Fc] * gt + cb_ref[:, c0:c0 + Fc])
        hid = (jax.nn.gelu(conv) * up).astype(bf16)
        acc = acc + _dot(hid, wout_ref[c0:c0 + Fc, :])
        for s in range(Bt):
            carry_s[s, :, c0:c0 + Fc] = gt[s * Lt + Lt - 2:s * Lt + Lt, :]
    y = _layer_norm(alpha * x + acc, g_ref[...], b_ref[...])
    for s in range(Bt):
        y_ref[s] = y[s * Lt:(s + 1) * Lt, :]
    nst_ref[...] = carry_s[...]


def _ffn(alpha, x3d, st, win, cw, cb, wout, g, b):
    B, L, _ = x3d.shape
    F = wout.shape[0]
    if L >= 256:
        Bt, Lt = 1, _row_tile(L, 512, 8)
    else:
        Bt, Lt = B, L
    n_chunks = 2
    blk = lambda shape: pl.BlockSpec((Bt,) + shape, lambda i, t: (i, t) + (0,) * (len(shape) - 1))
    st_spec = pl.BlockSpec((Bt, 2, F), lambda i, t: (i, 0, 0))
    return pl.pallas_call(
        functools.partial(_ffn_kernel, alpha, Bt, Lt, F, n_chunks),
        grid=(B // Bt, L // Lt),
        in_specs=[blk((Lt, D_MODEL)), st_spec, _const_spec((D_MODEL, 2 * F)), _const_spec((3, F)),
                  _const_spec((1, F)), _const_spec((F, D_MODEL)), _const_spec((1, D_MODEL)),
                  _const_spec((1, D_MODEL))],
        out_specs=[blk((Lt, D_MODEL)), st_spec],
        out_shape=[jax.ShapeDtypeStruct((B, L, D_MODEL), f32), jax.ShapeDtypeStruct((B, 2, F), f32)],
        scratch_shapes=[pltpu.VMEM((Bt, 2, F), f32)],
        compiler_params=_params("parallel", "arbitrary"),
        name="ffn",
    )(x3d, st, win, cw, cb, wout, g, b)


def _prep_layer_weights(l, P):
    W = GROUP_W
    w_in, b_in = P['w_in'][l], P['b_in'][l]
    o_fox = 2 * W
    o_ff = o_fox + 3 * W
    o_sb = o_ff + N_HEADS
    o_dn = o_sb + 3 * W
    o_da = o_dn + 3 * W
    o_db = o_da + N_HEADS
    o_dg = o_db + N_HEADS
    pad = SMALL_W - 3 * N_HEADS

    def reorder(a):
        parts = [a[..., 0:o_ff], a[..., o_sb:o_da], a[..., o_dg:o_dg + W], a[..., o_ff:o_ff + N_HEADS],
                 a[..., o_da:o_da + 2 * N_HEADS], jnp.zeros(a.shape[:-1] + (pad,), a.dtype)]
        return jnp.concatenate(parts, axis=-1)

    w_r = reorder(w_in).astype(bf16)
    b_r = reorder(b_in).reshape(1, -1)

    def block_diag(w4):
        eye = jnp.eye(N_HEADS, dtype=w4.dtype)
        return (w4[:, :, None, :] * eye[:, None, :, None]).reshape(W, W)

    small_row = lambda vals, off: jnp.zeros((1, SMALL_W), f32).at[0, off:off + N_HEADS].set(vals)
    r1 = lambda a: a.reshape(1, -1)
    return dict(
        w_in=w_r, b_in=b_r,
        lru_cw=P['lru_conv_w'][l], lru_cb=r1(P['lru_conv_b'][l]),
        lru_wa=block_diag(P['lru_w_a'][l]).astype(bf16), lru_wx=block_diag(P['lru_w_x'][l]).astype(bf16),
        lru_ba=r1(P['lru_b_a'][l]), lru_bx=r1(P['lru_b_x'][l]), lru_lam=r1(P['lru_lambda'][l]),
        gains=[r1(P['grp_norm_g'][l][i]) for i in range(3)],
        dn_cw=P['dn_conv_w'][l], dn_alog=small_row(P['dn_a_log'][l], N_HEADS),
        dn_dtb=small_row(P['dn_dt_bias'][l], N_HEADS), dn_ng=r1(jnp.tile(P['dn_norm_g'][l], N_HEADS)),
        w_o=P['w_o'][l].astype(bf16), ln1_g=r1(P['ln1_g'][l]), ln1_b=r1(P['ln1_b'][l]),
        ffn_win=P['ffn_w_in'][l].astype(bf16), ffn_cw=P['ffn_conv_w'][l], ffn_cb=r1(P['ffn_conv_b'][l]),
        ffn_wout=P['ffn_w_out'][l].astype(bf16), ln2_g=r1(P['ln2_g'][l]), ln2_b=r1(P['ln2_b'][l]),
    )


def _embed_block_diag(S):
    B = S.shape[0]
    eye = jnp.eye(N_HEADS, dtype=S.dtype)
    return (S[:, :, :, None, :] * eye[None, :, None, :, None]).reshape(B, GROUP_W, GROUP_W)


def _extract_block_diag(Sbd):
    return jnp.stack([Sbd[:, h * HEAD_DIM:(h + 1) * HEAD_DIM, h * HEAD_DIM:(h + 1) * HEAD_DIM]
                      for h in range(N_HEADS)], axis=1)


def _trunk(x, st, LW, ln_in, alpha):
    B, L, D = x.shape
    W = GROUP_W
    F = LW[0]['ffn_wout'].shape[0]
    new = {n: [] for n in ('lru_conv', 'lru_h', 'fox_k', 'fox_v', 'fox_logf', 'sb_k', 'sb_v',
                           'dn_conv', 'dn_S', 'ffn_conv')}
    x2 = x.reshape(B * L, D)
    for l, w in enumerate(LW):
        outs = _proj(x2, ln_in[0], ln_in[1], w['w_in'], w['b_in'], do_ln=(l == 0))
        if l == 0:
            x2, outs = outs[0], outs[1:]
        lru, fq, fk, fv, sq, sk, sv, dqkv, dgate, small = [o.reshape(B, L, -1) for o in outs]
        if st is None:
            lru_conv = jnp.zeros((B, 3, W), f32)
            lru_h = jnp.zeros((B, 1, W), f32)
            dn_conv = jnp.zeros((B, 3, 3 * W), f32)
            dn_S = jnp.zeros((B, W, W), f32)
            ffn_conv = jnp.zeros((B, 2, F), f32)
            pfk = pfv = plf = psk = psv = None
        else:
            Pn = st['fox_k'].shape[2]
            lru_conv = st['lru_conv'][l]
            lru_h = st['lru_h'][l].reshape(B, 1, W)
            dn_conv = st['dn_conv'][l]
            dn_S = _embed_block_diag(st['dn_S'][l])
            ffn_conv = st['ffn_conv'][l]
            pfk = st['fox_k'][l].reshape(B, Pn, W)
            pfv = st['fox_v'][l].reshape(B, Pn, W)
            plf = jnp.pad(st['fox_logf'][l], ((0, 0), (0, 0), (0, SMALL_W - N_HEADS)))
            psk = st['sb_k'][l].reshape(B, Pn, W)
            psv = st['sb_v'][l].reshape(B, Pn, W)

        ya, lru_conv_new, h_last = _lru(lru, lru_conv, lru_h, w['lru_cw'], w['lru_cb'], w['lru_wa'],
                                        w['lru_wx'], w['lru_ba'], w['lru_bx'], w['lru_lam'], w['gains'][0])
        yb, logf = _fox(fq, fk, fv, small, pfk, pfv, plf, w['gains'][1])
        yc = _sb(sq, sk, sv, psk, psv, w['gains'][2])
        yd, dn_conv_new, S_new = _dn(dqkv, dgate, small, dn_conv, dn_S, w['dn_cw'], w['dn_alog'],
                                     w['dn_dtb'], w['dn_ng'])
        x1 = _merge(alpha, x2, ya.reshape(B * L, W), yb.reshape(B * L, W), yc.reshape(B * L, W),
                    yd.reshape(B * L, W), w['w_o'], w['ln1_g'], w['ln1_b'])
        x3, ffn_new = _ffn(alpha, x1.reshape(B, L, D), ffn_conv, w['ffn_win'], w['ffn_cw'], w['ffn_cb'],
                           w['ffn_wout'], w['ln2_g'], w['ln2_b'])
        x2 = x3.reshape(B * L, D)

        heads = lambda a: a.reshape(B, L, N_HEADS, HEAD_DIM)
        new['lru_conv'].append(lru_conv_new)
        new['lru_h'].append(h_last.reshape(B, W))
        new['fox_k'].append(heads(fk))
        new['fox_v'].append(heads(fv))
        new['fox_logf'].append(logf[:, :, :N_HEADS])
        new['sb_k'].append(heads(sk))
        new['sb_v'].append(heads(sv))
        new['dn_conv'].append(dn_conv_new)
        new['dn_S'].append(_extract_block_diag(S_new))
        new['ffn_conv'].append(ffn_new)
    return x2.reshape(B, L, D), {n: jnp.stack(v) for n, v in new.items()}


def kernel(x_prompt, x_sample, state_lru_conv, state_lru_h, cache_fox_k, cache_fox_v, cache_fox_logf,
           cache_sb_k, cache_sb_v, state_dn_conv, state_dn_S, state_ffn_conv, meta_tokens, ln_in_g, ln_in_b,
           w_in, b_in, lru_conv_w, lru_conv_b, lru_w_a, lru_b_a, lru_w_x, lru_b_x, lru_lambda, dn_conv_w,
           dn_a_log, dn_dt_bias, dn_norm_g, grp_norm_g, w_o, ln1_g, ln1_b, ffn_w_in, ffn_conv_w, ffn_conv_b,
           ffn_w_out, ln2_g, ln2_b):
    P = dict(w_in=w_in, b_in=b_in, lru_conv_w=lru_conv_w, lru_conv_b=lru_conv_b, lru_w_a=lru_w_a,
             lru_b_a=lru_b_a, lru_w_x=lru_w_x, lru_b_x=lru_b_x, lru_lambda=lru_lambda, dn_conv_w=dn_conv_w,
             dn_a_log=dn_a_log, dn_dt_bias=dn_dt_bias, dn_norm_g=dn_norm_g, grp_norm_g=grp_norm_g, w_o=w_o,
             ln1_g=ln1_g, ln1_b=ln1_b, ffn_w_in=ffn_w_in, ffn_conv_w=ffn_conv_w, ffn_conv_b=ffn_conv_b,
             ffn_w_out=ffn_w_out, ln2_g=ln2_g, ln2_b=ln2_b)
    depth = w_in.shape[0]
    assert x_prompt.shape[2] == D_MODEL
    alpha = (2.0 * depth) ** 0.25
    LW = [_prep_layer_weights(l, P) for l in range(depth)]
    ln_in = (ln_in_g.reshape(1, -1), ln_in_b.reshape(1, -1))

    bp = x_prompt.shape[0]
    meta = jnp.broadcast_to(meta_tokens.astype(x_prompt.dtype), (bp, N_META, D_MODEL))
    xp = jnp.concatenate([meta, x_prompt], axis=1)
    yp, ps = _trunk(xp, None, LW, ln_in, alpha)

    st_in = dict(lru_conv=state_lru_conv, lru_h=state_lru_h, fox_k=cache_fox_k, fox_v=cache_fox_v,
                 fox_logf=cache_fox_logf, sb_k=cache_sb_k, sb_v=cache_sb_v, dn_conv=state_dn_conv,
                 dn_S=state_dn_S, ffn_conv=state_ffn_conv)
    ys, ss = _trunk(x_sample, st_in, LW, ln_in, alpha)

    names = ('lru_conv', 'lru_h', 'fox_k', 'fox_v', 'fox_logf', 'sb_k', 'sb_v', 'dn_conv', 'dn_S', 'ffn_conv')
    return (yp[:, N_META:], ys) + tuple(ps[n] for n in names) + tuple(ss[n] for n in names)
```

```python
import functools

import jax
import jax.numpy as jnp
from jax import lax
from jax.experimental import pallas as pl
from jax.experimental.pallas import tpu as pltpu

f32 = jnp.float32
bf16 = jnp.bfloat16

D_MODEL = 1024
GROUP_W = 256
N_HEADS = 4
HEAD_DIM = 64
N_META = 16
LRU_C = 8.0
LN_EPS = 1e-5
RMS_EPS = 1e-6
ATT_TILE = 256
ATT_SUB = 128
DN_CHUNK = 64
NEG = -1e30
LOG2E = 1.4426950408889634
SMALL_W = 128
VMEM_LIMIT_BYTES = 56 * 1024 * 1024

NN = (((1,), (0,)), ((), ()))
NT = (((1,), (1,)), ((), ()))
TN = (((0,), (0,)), ((), ()))


def _dot(a, b, dims=NN):
    return lax.dot_general(a, b, dims, preferred_element_type=f32)


def _split2(x):
    hi = x.astype(bf16)
    lo = (x - hi.astype(f32)).astype(bf16)
    return hi, lo


def _split3(x):
    p0 = x.astype(bf16)
    r = x - p0.astype(f32)
    p1 = r.astype(bf16)
    p2 = (r - p1.astype(f32)).astype(bf16)
    return p0, p1, p2


def _dot_sel_rhs(x, sel, dims=NN):
    p0, p1, p2 = _split3(x)
    return _dot(p0, sel, dims) + _dot(p1, sel, dims) + _dot(p2, sel, dims)


def _dot_sel_lhs(sel, x, dims=NN):
    p0, p1, p2 = _split3(x)
    return _dot(sel, p0, dims) + _dot(sel, p1, dims) + _dot(sel, p2, dims)


def _iota(shape, dim):
    return lax.broadcasted_iota(jnp.int32, shape, dim)


def _head_masks(width=GROUP_W):
    lane_head = _iota((1, width), 1) >> 6
    return [lane_head == h for h in range(N_HEADS)]


def _softplus(x):
    return jnp.maximum(x, 0.0) + jnp.log1p(jnp.exp(-jnp.abs(x)))


def _log_sigmoid(x):
    return jnp.minimum(x, 0.0) - jnp.log1p(jnp.exp(-jnp.abs(x)))


def _layer_norm(x, g, b):
    mu = jnp.mean(x, -1, keepdims=True)
    xc = x - mu
    var = jnp.mean(xc * xc, -1, keepdims=True)
    return xc * lax.rsqrt(var + LN_EPS) * g + b


def _rms_norm(x, g):
    return x * lax.rsqrt(jnp.mean(x * x, -1, keepdims=True) + RMS_EPS) * g


def _row_tile(n, cap, mult=16):
    best = None
    for d in range(mult, min(n, cap) + 1, mult):
        if n % d == 0:
            best = d
    return best if best is not None else n


def _const_spec(shape):
    nd = len(shape)
    return pl.BlockSpec(shape, lambda *_: (0,) * nd, pipeline_mode=pl.Buffered(1))


def _params(*sem):
    return pltpu.CompilerParams(dimension_semantics=sem, vmem_limit_bytes=VMEM_LIMIT_BYTES)


PROJ_WIDTHS = (512, 256, 256, 256, 256, 256, 256, 768, 256, SMALL_W)


def _proj_kernel(do_ln, x_ref, g_ref, b_ref, w_ref, bias_ref, *outs):
    x = x_ref[...]
    if do_ln:
        x = _layer_norm(x, g_ref[...], b_ref[...])
        outs[0][...] = x
        outs = outs[1:]
    xb = x.astype(bf16)
    col = 0
    for o, wd in zip(outs, PROJ_WIDTHS):
        o[...] = _dot(xb, w_ref[col:col + wd, :], NT) + bias_ref[:, col:col + wd]
        col += wd


def _proj(x2d, ln_g, ln_b, w, bias, do_ln):
    n = x2d.shape[0]
    tm = _row_tile(n, 768)
    wtot = w.shape[0]
    row = lambda wd: pl.BlockSpec((tm, wd), lambda i: (i, 0))
    out_shape = [jax.ShapeDtypeStruct((n, wd), f32) for wd in PROJ_WIDTHS]
    out_specs = [row(wd) for wd in PROJ_WIDTHS]
    if do_ln:
        out_shape = [jax.ShapeDtypeStruct((n, D_MODEL), f32)] + out_shape
        out_specs = [row(D_MODEL)] + out_specs
    return pl.pallas_call(
        functools.partial(_proj_kernel, do_ln),
        grid=(n // tm,),
        in_specs=[row(D_MODEL), _const_spec((1, D_MODEL)), _const_spec((1, D_MODEL)),
                  _const_spec((wtot, D_MODEL)), _const_spec((1, wtot))],
        out_specs=out_specs,
        out_shape=out_shape,
        compiler_params=_params("parallel"),
        name="proj",
    )(x2d, ln_g, ln_b, w, bias)


def _lru_kernel(L, R1, lru_ref, cst_ref, h0_ref, cw_ref, cb_ref, wa_ref, wx_ref, ba_ref, bx_ref,
                lam_ref, gain_ref, ya_ref, cnew_ref, hlast_ref, xpad_s, a_s, b_s):
    W = GROUP_W
    xpad_s[pl.ds(0, 8), :] = jnp.zeros((8, W), f32)
    xpad_s[pl.ds(5, 3), :] = cst_ref[0]
    xpad_s[pl.ds(8, L), :] = lru_ref[0, :, 0:W]
    cnew_ref[0] = xpad_s[pl.ds(L + 5, 3), :]

    sp = _softplus(-lam_ref[...])
    cw = cw_ref[...]
    for t in range(L // R1):
        r0 = t * R1
        xa = cb_ref[...]
        for i in range(4):
            xa = xa + cw[i:i + 1, :] * xpad_s[pl.ds(5 + r0 + i, R1), :]
        xb = xa.astype(bf16)
        r = jax.nn.sigmoid(_dot(xb, wa_ref[...]) + ba_ref[...])
        ig = jax.nn.sigmoid(_dot(xb, wx_ref[...]) + bx_ref[...])
        log_a = (-LRU_C) * r * sp
        y2 = 2.0 * log_a
        one_m_a2 = jnp.tanh(-0.5 * y2) * (1.0 + jnp.exp(y2))
        a_s[pl.ds(r0, R1), :] = jnp.exp(log_a)
        b_s[pl.ds(r0, R1), :] = jnp.sqrt(one_m_a2) * (ig * xa)

    rows8 = _iota((8, W), 0)

    def scan_body(g, hprev):
        off = pl.multiple_of(g * 8, 8)
        A = a_s[pl.ds(off, 8), :]
        Bv = b_s[pl.ds(off, 8), :]
        for s in (1, 2, 4):
            keep = rows8 >= s
            a_sh = jnp.where(keep, pltpu.roll(A, s, 0), 1.0)
            b_sh = jnp.where(keep, pltpu.roll(Bv, s, 0), 0.0)
            Bv = A * b_sh + Bv
            A = A * a_sh
        h = A * hprev + Bv
        b_s[pl.ds(off, 8), :] = h
        return h[7:8, :]

    hlast_ref[0] = lax.fori_loop(0, L // 8, scan_body, h0_ref[0])

    for t in range(L // R1):
        r0 = t * R1
        o = jax.nn.gelu(lru_ref[0, pl.ds(r0, R1), W:2 * W]) * b_s[pl.ds(r0, R1), :]
        ya_ref[0, pl.ds(r0, R1), :] = _rms_norm(o, gain_ref[...]).astype(bf16)


def _lru(lru3d, conv_st, h0, cw, cb, wa, wx, ba, bx, lam, gain):
    B, L, _ = lru3d.shape
    R1 = _row_tile(L, 768)
    W = GROUP_W
    per_b = lambda shape: pl.BlockSpec((1,) + shape, lambda b: (b,) + (0,) * len(shape))
    return pl.pallas_call(
        functools.partial(_lru_kernel, L, R1),
        grid=(B,),
        in_specs=[per_b((L, 2 * W)), per_b((3, W)), per_b((1, W)),
                  _const_spec((4, W)), _const_spec((1, W)), _const_spec((W, W)), _const_spec((W, W)),
                  _const_spec((1, W)), _const_spec((1, W)), _const_spec((1, W)), _const_spec((1, W))],
        out_specs=[per_b((L, W)), per_b((3, W)), per_b((1, W))],
        out_shape=[jax.ShapeDtypeStruct((B, L, W), bf16), jax.ShapeDtypeStruct((B, 3, W), f32),
                   jax.ShapeDtypeStruct((B, 1, W), f32)],
        scratch_shapes=[pltpu.VMEM((L + 8, W), f32), pltpu.VMEM((L, W), f32), pltpu.VMEM((L, W), f32)],
        compiler_params=_params("parallel"),
        name="lru",
    )(lru3d, conv_st, h0, cw, cb, wa, wx, ba, bx, lam, gain)


def _att_layout(L, P):
    T = ATT_TILE
    assert (P + L) % 16 == 0
    front = (-(P + L)) % T
    Lpad = front + P + L
    L0 = L % T
    nT = L // T
    base_t = (front + P + L0) // T
    return front, Lpad, L0, nT, base_t


def _stage_pairs(front, P, L, new_ref, past_ref, flat_s, pair_s, hm):
    W = GROUP_W
    S = ATT_SUB
    if front:
        flat_s[pl.ds(0, front), :] = jnp.zeros((front, W), bf16)
    if P:
        assert P % S == 0

        def past_body(u, c):
            r = pl.multiple_of(u * S, S)
            flat_s[pl.ds(pl.multiple_of(front + r, 16), S), :] = past_ref[0, :, pl.ds(r, S)].T.astype(bf16)
            return c

        lax.fori_loop(0, P // S, past_body, 0)
    flat_s[pl.ds(front + P, L), :] = new_ref[0].astype(bf16)
    zero = jnp.zeros((S, W), bf16)

    def body(u, c):
        x = flat_s[pl.ds(pl.multiple_of(u * S, S), S), :]
        base = pl.multiple_of(u * 2 * S, 2 * S)
        for p in range(2):
            pair_s[p, pl.ds(base, S), :] = jnp.where(hm[2 * p], x, zero)
            pair_s[p, pl.ds(base + S, S), :] = jnp.where(hm[2 * p + 1], x, zero)
        return c

    lax.fori_loop(0, (front + P + L) // S, body, 0)


def _store_transposed(src_ref, dst_ref, L):
    S = ATT_SUB
    n_full = L // S

    def body(u, c):
        r = pl.multiple_of(u * S, S)
        dst_ref[0, :, pl.ds(r, S)] = src_ref[0, pl.ds(r, S), :].T
        return c

    lax.fori_loop(0, n_full, body, 0)
    tail = L - n_full * S
    if tail:
        eye = (_iota((GROUP_W, GROUP_W), 0) == _iota((GROUP_W, GROUP_W), 1)).astype(bf16)
        dst_ref[0, :, pl.ds(n_full * S, tail)] = _dot_sel_lhs(eye, src_ref[0, pl.ds(n_full * S, tail), :], NT)


def _pair_scores(qb, kpair_s, j):
    S2 = 2 * ATT_SUB
    kbase = j * (2 * ATT_TILE)
    return [[_dot(qb, kpair_s[p, pl.ds(pl.multiple_of(kbase + sub * S2, S2), S2), :], NT)
             for sub in range(2)] for p in range(2)]


def _pair_pv(wb, vpair_s, j):
    S2 = 2 * ATT_SUB
    kbase = j * (2 * ATT_TILE)
    out = None
    for sub in range(2):
        for p in range(2):
            lhs = jnp.concatenate([wb[2 * p][sub], wb[2 * p + 1][sub]], axis=1)
            t = _dot(lhs, vpair_s[p, pl.ds(pl.multiple_of(kbase + sub * S2, S2), S2), :])
            out = t if out is None else out + t
    return out


def _by_head_lanes(vals, lo_half):
    return jnp.concatenate([jnp.where(lo_half, vals[0], vals[1]), jnp.where(lo_half, vals[2], vals[3])], axis=1)


def _fox_kernel(L, P, emit_t, *refs):
    T, S = ATT_TILE, ATT_SUB
    front, Lpad, L0, nT, base_t = _att_layout(L, P)
    n_in = 8 if P else 5
    n_out = 4 if emit_t else 2
    if P:
        q_ref, k_ref, v_ref, sm_ref, pk_ref, pv_ref, plf_ref, gain_ref = refs[:n_in]
    else:
        q_ref, k_ref, v_ref, sm_ref, gain_ref = refs[:n_in]
        pk_ref = pv_ref = plf_ref = None
    yb_ref, lf_ref = refs[n_in:n_in + 2]
    scr = refs[n_in + n_out:]
    flat_s, kpair_s, vpair_s, lfp_s, F_s, FT_s, qb_s, fqb_s, m_s, lp_s, acc_s, cb_s = scr
    hm = _head_masks()
    lo_half = _iota((1, S), 1) < HEAD_DIM
    scale = HEAD_DIM ** -0.5 * LOG2E
    if emit_t:
        _store_transposed(k_ref, refs[n_in + 2], L)
        _store_transposed(v_ref, refs[n_in + 3], L)

    _stage_pairs(front, P, L, k_ref, pk_ref, flat_s, kpair_s, hm)
    _stage_pairs(front, P, L, v_ref, pv_ref, flat_s, vpair_s, hm)

    if front:
        lfp_s[pl.ds(0, front), :] = jnp.zeros((front, SMALL_W), f32)
    if P:
        lfp_s[pl.ds(front, P), :] = plf_ref[0]
    lf = _log_sigmoid(sm_ref[0])
    lf_ref[0] = lf
    lfp_s[pl.ds(front + P, L), :] = lf
    ltri = (_iota((S, S), 0) >= _iota((S, S), 1)).astype(bf16)
    sel8 = (_iota((8, SMALL_W), 0) == _iota((8, SMALL_W), 1)).astype(bf16)

    def cum_body(u, carry):
        r = pl.multiple_of(u * S, S)
        F = _dot_sel_lhs(ltri, lfp_s[pl.ds(r, S), :]) + carry
        F2 = F * LOG2E
        F_s[pl.ds(r, S), :] = F2
        FT_s[:, pl.ds(r, S)] = _dot_sel_lhs(sel8, F2, NT)
        return F[S - 1:S, :]

    lax.fori_loop(0, Lpad // S, cum_body, jnp.zeros((1, SMALL_W), f32))
    if front:
        FT_s[:, pl.ds(0, front)] = jnp.full((8, front), -NEG, f32)
    cb_s[...] = jnp.where(_iota((T, T), 1) <= _iota((T, T), 0), 0.0, NEG)

    def process(qnat, tq, d):
        rows = pl.ds(0, tq)
        qp0 = front + P + qnat
        qb_s[rows, :] = (q_ref[0, pl.ds(qnat, tq), :] * scale).astype(bf16)
        Fq = F_s[pl.ds(qp0, tq), :]
        for h in range(N_HEADS):
            fqb_s[h, rows, :] = jnp.broadcast_to(Fq[:, h:h + 1], (tq, S))
            m_s[h, rows, :] = jnp.full((tq, S), NEG, f32)
            lp_s[h, rows, :] = jnp.zeros((tq, S), f32)
        acc_s[rows, :] = jnp.zeros((tq, GROUP_W), f32)

        def scores(j):
            return _pair_scores(qb_s[rows, :], kpair_s, j)

        def update(j, sc, mode):
            if mode == 'ragged':
                qpos = qp0 + _iota((tq, S), 0)
                vis = []
                for sub in range(2):
                    kpos = j * T + sub * S + _iota((tq, S), 1)
                    ok = kpos <= qpos
                    if front:
                        ok = ok & (kpos >= front)
                    vis.append(ok)
            alphas, pb = [], []
            for h in range(N_HEADS):
                p_, half = divmod(h, 2)
                fq = fqb_s[h, rows, :]
                c = []
                for sub in range(2):
                    fk = FT_s[h:h + 1, pl.ds(pl.multiple_of(j * T + sub * S, S), S)]
                    x = sc[p_][sub][:, half * S:(half + 1) * S] + (fq - fk)
                    if mode == 'ragged':
                        x = jnp.where(vis[sub], x, NEG)
                    elif mode == 'diag':
                        x = x + cb_s[:, sub * S:(sub + 1) * S]
                    c.append(x)
                m_old = m_s[h, rows, :]
                m_new = jnp.maximum(m_old, jnp.max(jnp.maximum(c[0], c[1]), -1, keepdims=True))
                alpha = jnp.exp2(m_old - m_new)
                e0 = jnp.exp2(c[0] - m_new)
                e1 = jnp.exp2(c[1] - m_new)
                lp_s[h, rows, :] = alpha * lp_s[h, rows, :] + (e0 + e1)
                m_s[h, rows, :] = m_new
                alphas.append(alpha)
                pb.append((e0.astype(bf16), e1.astype(bf16)))
            pv = _pair_pv(pb, vpair_s, j)
            acc_s[rows, :] = acc_s[rows, :] * _by_head_lanes(alphas, lo_half) + pv

        def pair(j0, mode0, j1, mode1):
            sc0 = scores(j0)
            sc1 = scores(j1)
            update(j0, sc0, mode0)
            update(j1, sc1, mode1)

        def plain2(t, c):
            pair(2 * t, 'plain', 2 * t + 1, 'plain')
            return c

        lax.fori_loop(0, d // 2, plain2, 0)
        if isinstance(d, int):
            if d % 2:
                pair(d - 1, 'plain', d, 'ragged')
            else:
                update(d, scores(d), 'ragged')
        else:
            @pl.when(d % 2 == 1)
            def _():
                pair(d - 1, 'plain', d, 'diag')

            @pl.when(d % 2 == 0)
            def _():
                update(d, scores(d), 'diag')

        l = [jnp.sum(lp_s[h, rows, :], -1, keepdims=True) for h in range(N_HEADS)]
        out = acc_s[rows, :] / _by_head_lanes(l, lo_half)
        yb_ref[0, pl.ds(qnat, tq), :] = _rms_norm(out, gain_ref[...]).astype(bf16)

    if L0:
        process(0, L0, base_t - 1)

    def q_body(i, c):
        process(pl.multiple_of(L0 + i * T, 16), T, base_t + i)
        return c

    lax.fori_loop(0, nT, q_body, 0)


def _fox(q, k, v, small, past_k, past_v, past_lf, gain, emit_t, layer):
    B, L, W = q.shape
    P = 0 if past_k is None else past_k.shape[3]
    T = ATT_TILE
    _, Lpad, _, _, _ = _att_layout(L, P)
    per_b = lambda shape: pl.BlockSpec((1,) + shape, lambda b: (b,) + (0,) * len(shape))
    per_lb = lambda shape: pl.BlockSpec((None, 1) + shape, lambda b: (layer, b) + (0,) * len(shape))
    in_specs = [per_b((L, W)), per_b((L, W)), per_b((L, W)), per_b((L, SMALL_W))]
    args = [q, k, v, small]
    if P:
        in_specs += [per_lb((W, P)), per_lb((W, P)), per_b((P, SMALL_W))]
        args += [past_k, past_v, past_lf]
    in_specs.append(_const_spec((1, W)))
    args.append(gain)
    out_specs = [per_b((L, W)), per_b((L, SMALL_W))]
    out_shape = [jax.ShapeDtypeStruct((B, L, W), bf16), jax.ShapeDtypeStruct((B, L, SMALL_W), f32)]
    if emit_t:
        out_specs += [per_b((W, L)), per_b((W, L))]
        out_shape += [jax.ShapeDtypeStruct((B, W, L), f32)] * 2
    return pl.pallas_call(
        functools.partial(_fox_kernel, L, P, emit_t),
        grid=(B,),
        in_specs=in_specs,
        out_specs=out_specs,
        out_shape=out_shape,
        scratch_shapes=[pltpu.VMEM((Lpad, W), bf16), pltpu.VMEM((2, 2 * Lpad, W), bf16),
                        pltpu.VMEM((2, 2 * Lpad, W), bf16), pltpu.VMEM((Lpad, SMALL_W), f32),
                        pltpu.VMEM((Lpad, SMALL_W), f32), pltpu.VMEM((8, Lpad), f32),
                        pltpu.VMEM((T, W), bf16), pltpu.VMEM((N_HEADS, T, ATT_SUB), f32),
                        pltpu.VMEM((N_HEADS, T, ATT_SUB), f32), pltpu.VMEM((N_HEADS, T, ATT_SUB), f32),
                        pltpu.VMEM((T, W), f32), pltpu.VMEM((T, T), f32)],
        compiler_params=_params("parallel"),
        name="fox",
    )(*args)


def _sb_kernel(L, P, emit_t, *refs):
    T, S = ATT_TILE, ATT_SUB
    front, Lpad, L0, nT, base_t = _att_layout(L, P)
    n_in = 6 if P else 4
    n_out = 3 if emit_t else 1
    if P:
        q_ref, k_ref, v_ref, pk_ref, pv_ref, gain_ref = refs[:n_in]
    else:
        q_ref, k_ref, v_ref, gain_ref = refs[:n_in]
        pk_ref = pv_ref = None
    yc_ref = refs[n_in]
    scr = refs[n_in + n_out:]
    flat_s, kpair_s, vpair_s, qb_s, c_s, acc_s, cb_s = scr
    hm = _head_masks()
    scale = HEAD_DIM ** -0.5 * LOG2E
    if emit_t:
        _store_transposed(k_ref, refs[n_in + 1], L)
        _store_transposed(v_ref, refs[n_in + 2], L)

    _stage_pairs(front, P, L, k_ref, pk_ref, flat_s, kpair_s, hm)
    _stage_pairs(front, P, L, v_ref, pv_ref, flat_s, vpair_s, hm)

    after2 = ((_iota((2 * T, T), 0) & (T - 1)) > _iota((2 * T, T), 1)).astype(bf16)
    cb_s[...] = jnp.where(_iota((T, T), 1) < _iota((T, T), 0), 0.0, NEG)
    pad_bias = jnp.where(_iota((1, T), 1) >= front, 0.0, NEG)

    def process(qnat, tq, d):
        rows = pl.ds(0, tq)
        qp0 = front + P + qnat
        qb_s[rows, :] = (q_ref[0, pl.ds(qnat, tq), :] * scale).astype(bf16)
        for h in range(N_HEADS):
            c_s[h, rows, :] = jnp.zeros((tq, S), f32)
        acc_s[rows, :] = jnp.zeros((tq, GROUP_W), f32)

        def scores(j):
            return _pair_scores(qb_s[rows, :], kpair_s, j)

        def update(j, sc, mode):
            if mode == 'ragged':
                kpos = j * T + _iota((tq, T), 1)
                vis = kpos < qp0 + _iota((tq, T), 0)
                if front:
                    vis = vis & (kpos >= front)
            wb = []
            for h in range(N_HEADS):
                p_, half = divmod(h, 2)
                z = jnp.concatenate([sc[p_][sub][:, half * S:(half + 1) * S] for sub in range(2)], axis=1)
                if mode == 'diag':
                    z = z + cb_s[...]
                elif mode == 'pad':
                    z = z + pad_bias
                ls = jnp.minimum(z, 0.0) - jnp.log2(1.0 + jnp.exp2(-jnp.abs(z)))
                lk = ls - z
                if mode == 'ragged':
                    lk = jnp.where(vis, lk, 0.0)
                hi, lo = _split2(lk)
                later = _dot(jnp.concatenate([hi, lo], axis=1), after2)
                c = c_s[h, rows, :]
                w = jnp.exp2(ls + later + jnp.concatenate([c, c], axis=1))
                if mode == 'ragged':
                    w = jnp.where(vis, w, 0.0)
                c_s[h, rows, :] = c + jnp.sum(lk, -1, keepdims=True)
                wbf = w.astype(bf16)
                wb.append((wbf[:, 0:S], wbf[:, S:2 * S]))
            acc_s[rows, :] = acc_s[rows, :] + _pair_pv(wb, vpair_s, j)

        def kv_step(j, mode):
            update(j, scores(j), mode)

        def plain2(t, c):
            j = d - 1 - 2 * t
            sc0 = scores(j)
            sc1 = scores(j - 1)
            update(j, sc0, 'plain')
            update(j - 1, sc1, 'plain')
            return c

        jf = 1 if front else 0
        kv_step(d, 'ragged' if isinstance(d, int) else 'diag')
        if (not isinstance(d, int)) or d > 0:
            n_plain = d - jf
            lax.fori_loop(0, n_plain // 2, plain2, 0)

            @pl.when(n_plain % 2 == 1)
            def _():
                kv_step(jf, 'plain')
            if front:
                kv_step(0, 'pad')

        yc_ref[0, pl.ds(qnat, tq), :] = _rms_norm(acc_s[rows, :], gain_ref[...]).astype(bf16)

    if L0:
        process(0, L0, base_t - 1)

    def q_body(i, c):
        process(pl.multiple_of(L0 + i * T, 16), T, base_t + i)
        return c

    lax.fori_loop(0, nT, q_body, 0)


def _sb(q, k, v, past_k, past_v, gain, emit_t, layer):
    B, L, W = q.shape
    P = 0 if past_k is None else past_k.shape[3]
    per_lb = lambda shape: pl.BlockSpec((None, 1) + shape, lambda b: (layer, b) + (0,) * len(shape))
    T = ATT_TILE
    _, Lpad, _, _, _ = _att_layout(L, P)
    per_b = lambda shape: pl.BlockSpec((1,) + shape, lambda b: (b,) + (0,) * len(shape))
    in_specs = [per_b((L, W)), per_b((L, W)), per_b((L, W))]
    args = [q, k, v]
    if P:
        in_specs += [per_lb((W, P)), per_lb((W, P))]
        args += [past_k, past_v]
    in_specs.append(_const_spec((1, W)))
    args.append(gain)
    out_specs = [per_b((L, W))]
    out_shape = [jax.ShapeDtypeStruct((B, L, W), bf16)]
    if emit_t:
        out_specs += [per_b((W, L)), per_b((W, L))]
        out_shape += [jax.ShapeDtypeStruct((B, W, L), f32)] * 2
    return pl.pallas_call(
        functools.partial(_sb_kernel, L, P, emit_t),
        grid=(B,),
        in_specs=in_specs,
        out_specs=out_specs,
        out_shape=out_shape,
        scratch_shapes=[pltpu.VMEM((Lpad, W), bf16), pltpu.VMEM((2, 2 * Lpad, W), bf16),
                        pltpu.VMEM((2, 2 * Lpad, W), bf16), pltpu.VMEM((T, W), bf16),
                        pltpu.VMEM((N_HEADS, T, ATT_SUB), f32), pltpu.VMEM((T, W), f32),
                        pltpu.VMEM((T, T), f32)],
        compiler_params=_params("parallel"),
        name="sb",
    )(*args)


def _dn_kernel(L, R1, qkv_ref, gate_ref, sm_ref, cst_ref, s0_ref, cw_ref, alog_ref, dtb_ref, ng_ref,
               yd_ref, cnew_ref, sout_ref, xpad_s, q_s, k_s, v_s, g_s, beta_s, o_s, S_s):
    W = GROUP_W
    hm = _head_masks()
    bd_ones = ((_iota((W, W), 0) >> 6) == (_iota((W, W), 1) >> 6)).astype(bf16)
    exp_g = (_iota((SMALL_W, W), 0) == (_iota((SMALL_W, W), 1) >> 6) + N_HEADS).astype(bf16)
    exp_b = (_iota((SMALL_W, W), 0) == (_iota((SMALL_W, W), 1) >> 6) + 2 * N_HEADS).astype(bf16)

    def head_sum(x):
        hi, lo = _split2(x)
        return _dot(hi, bd_ones) + _dot(lo, bd_ones)

    xpad_s[pl.ds(0, 8), :] = jnp.zeros((8, 3 * W), f32)
    xpad_s[pl.ds(5, 3), :] = cst_ref[0]
    xpad_s[pl.ds(8, L), :] = qkv_ref[0]
    cnew_ref[0] = xpad_s[pl.ds(L + 5, 3), :]
    cw = cw_ref[...]
    neg_a = -jnp.exp(alog_ref[...])
    for t in range(L // R1):
        r0 = t * R1
        rows = pl.ds(r0, R1)
        xc = cw[0:1, :] * xpad_s[pl.ds(5 + r0, R1), :]
        for i in range(1, 4):
            xc = xc + cw[i:i + 1, :] * xpad_s[pl.ds(5 + r0 + i, R1), :]
        xc = xc * jax.nn.sigmoid(xc)
        q = xc[:, 0:W]
        k = xc[:, W:2 * W]
        q_s[rows, :] = q * lax.rsqrt(head_sum(q * q) + RMS_EPS) * (HEAD_DIM ** -0.5)
        k_s[rows, :] = k * lax.rsqrt(head_sum(k * k) + RMS_EPS)
        v_s[rows, :] = xc[:, 2 * W:3 * W]
        sm = sm_ref[0, rows, :]
        g_s[rows, :] = _dot_sel_rhs(neg_a * _softplus(sm + dtb_ref[...]), exp_g)
        beta_s[rows, :] = _dot_sel_rhs(jax.nn.sigmoid(sm), exp_b)

    S_s[...] = s0_ref[0]
    bd_mask = (_iota((W, W), 0) >> 6) == (_iota((W, W), 1) >> 6)

    def part_a(offs, C):
        each = lambda f, *ls: [f(*a) for a in zip(*ls)]
        CW = N_HEADS * C
        assert C & (C - 1) == 0
        ri = _iota((C, CW), 0)
        si = _iota((C, CW), 1) & (C - 1)
        blk = [(_iota((1, CW), 1) >> (C.bit_length() - 1)) == h for h in range(N_HEADS)]
        ltri = (_iota((C, C), 0) >= _iota((C, C), 1)).astype(bf16)

        def head_rows(x):
            xb = x.astype(bf16)
            return jnp.concatenate([jnp.where(hm[h], xb, jnp.zeros_like(xb)) for h in range(N_HEADS)], axis=0)

        def head_blocks(y):
            yb = y.astype(bf16)
            return jnp.concatenate([jnp.where(blk[h], yb, jnp.zeros_like(yb)) for h in range(N_HEADS)], axis=0)

        q = [q_s[pl.ds(off, C), :] for off in offs]
        k = [k_s[pl.ds(off, C), :] for off in offs]
        beta = [beta_s[pl.ds(off, C), :] for off in offs]
        gc = [_dot_sel_lhs(ltri, g_s[pl.ds(off, C), :]) for off in offs]
        eg = each(jnp.exp, gc)
        g_last = [g[C - 1:C, :] for g in gc]
        kb = each(lambda a, b: a * b, k, beta)
        vb = [v_s[pl.ds(off, C), :] * b for off, b in zip(offs, beta)]
        kbe = each(lambda a, b: a * b, kb, eg)

        gparts = each(_split3, gc)
        first_lane = ((_iota((C, W), 1) & (HEAD_DIM - 1)) == 0).astype(bf16)
        first3 = jnp.concatenate([first_lane] * 3, axis=1)
        g_row = [_dot(first3, jnp.concatenate([head_rows(p) for p in gp], axis=1), NT) for gp in gparts]
        if CW == W:
            g_col = gc
        else:
            spread = (_iota((W, CW), 0) == (_iota((W, CW), 1) >> (C.bit_length() - 1)) * HEAD_DIM).astype(bf16)
            spread3 = jnp.concatenate([spread] * 3, axis=0)
            g_col = [_dot(jnp.concatenate(gp, axis=1), spread3) for gp in gparts]
        decay = each(lambda c, r: jnp.exp(jnp.minimum(c - r, 0.0)), g_col, g_row)
        k_rows = each(head_rows, k)
        m = each(lambda a, kr, d: jnp.where(ri > si, _dot(a.astype(bf16), kr, NT) * d, 0.0), kb, k_rows, decay)
        a_in = each(lambda a, kr, d: jnp.where(ri >= si, _dot(a.astype(bf16), kr, NT) * d, 0.0).astype(bf16),
                    q, k_rows, decay)
        pw = [-a for a in m]
        x = pw
        pw_bd = each(head_blocks, pw)
        for _ in range((C - 1).bit_length() - 1):
            pw = each(lambda a, b: _dot(a.astype(bf16), b), pw, pw_bd)
            pw_bd = each(head_blocks, pw)
            x = each(lambda a, p, b: a + p + _dot(a.astype(bf16), b), x, pw, pw_bd)
        xb = [a.astype(bf16) for a in x]
        u = each(lambda a, b: b + _dot(a, head_rows(b)), xb, vb)
        w = each(lambda a, b: (b + _dot(a, head_rows(b))).astype(bf16), xb, kbe)
        qe = each(lambda a, b: (a * b).astype(bf16), q, eg)
        kd = each(lambda a, gl, g: (a * jnp.exp(gl - g)).astype(bf16), k, g_last, gc)
        return list(zip(u, w, a_in, qe, kd, each(jnp.exp, g_last)))

    def part_b(off, C, u, wb, a_in, qe, kd, eg_last):
        S = S_s[...]
        Sb = S.astype(bf16)
        v_new = u - _dot(wb, Sb)
        vnb = v_new.astype(bf16)
        v_rows = jnp.concatenate([jnp.where(hm[h], vnb, jnp.zeros_like(vnb)) for h in range(N_HEADS)], axis=0)
        o_s[pl.ds(off, C), :] = _dot(qe, Sb) + _dot(a_in, v_rows)
        S_s[...] = S * eg_last + jnp.where(bd_mask, _dot(kd, vnb, TN), 0.0)

    C0 = L % DN_CHUNK
    if C0:
        part_b(0, C0, *part_a([0], C0)[0])

    n_blocks = L // DN_CHUNK
    group = max(g for g in (8, 4, 2, 1) if n_blocks % g == 0)

    def group_body(c, carry):
        offs = [pl.multiple_of(C0 + (c * group + i) * DN_CHUNK, 16) for i in range(group)]
        for off, a in zip(offs, part_a(offs, DN_CHUNK)):
            part_b(off, DN_CHUNK, *a)
        return carry

    lax.fori_loop(0, n_blocks // group, group_body, 0)
    sout_ref[0] = S_s[...]

    for t in range(L // R1):
        rows = pl.ds(t * R1, R1)
        o = o_s[rows, :]
        gate = gate_ref[0, rows, :]
        y = o * lax.rsqrt(head_sum(o * o) * (1.0 / HEAD_DIM) + RMS_EPS) * ng_ref[...]
        yd_ref[0, rows, :] = (y * (gate * jax.nn.sigmoid(gate))).astype(bf16)


def _dn(qkv, gate, small, conv_st, s0_bd, cw, alog_row, dtb_row, ng_row):
    B, L, _ = qkv.shape
    W = GROUP_W
    R1 = _row_tile(L, 768)
    per_b = lambda shape: pl.BlockSpec((1,) + shape, lambda b: (b,) + (0,) * len(shape))
    return pl.pallas_call(
        functools.partial(_dn_kernel, L, R1),
        grid=(B,),
        in_specs=[per_b((L, 3 * W)), per_b((L, W)), per_b((L, SMALL_W)), per_b((3, 3 * W)), per_b((W, W)),
                  _const_spec((4, 3 * W)), _const_spec((1, SMALL_W)), _const_spec((1, SMALL_W)),
                  _const_spec((1, W))],
        out_specs=[per_b((L, W)), per_b((3, 3 * W)), per_b((W, W))],
        out_shape=[jax.ShapeDtypeStruct((B, L, W), bf16), jax.ShapeDtypeStruct((B, 3, 3 * W), f32),
                   jax.ShapeDtypeStruct((B, W, W), f32)],
        scratch_shapes=[pltpu.VMEM((L + 8, 3 * W), f32)] + [pltpu.VMEM((L, W), f32)] * 6
                       + [pltpu.VMEM((W, W), f32)],
        compiler_params=_params("parallel"),
        name="dn",
    )(qkv, gate, small, conv_st, s0_bd, cw, alog_row, dtb_row, ng_row)


def _merge_kernel(alpha, x_ref, ya_ref, yb_ref, yc_ref, yd_ref, wo_ref, g_ref, b_ref, o_ref):
    W = GROUP_W
    m = _dot(ya_ref[...], wo_ref[0:W, :])
    m = m + _dot(yb_ref[...], wo_ref[W:2 * W, :])
    m = m + _dot(yc_ref[...], wo_ref[2 * W:3 * W, :])
    m = m + _dot(yd_ref[...], wo_ref[3 * W:4 * W, :])
    o_ref[...] = _layer_norm(alpha * x_ref[...] + m, g_ref[...], b_ref[...])


def _merge(alpha, x2d, ya, yb, yc, yd, wo, g, b):
    n = x2d.shape[0]
    tm = _row_tile(n, 768)
    row = lambda wd: pl.BlockSpec((tm, wd), lambda i: (i, 0))
    return pl.pallas_call(
        functools.partial(_merge_kernel, alpha),
        grid=(n // tm,),
        in_specs=[row(D_MODEL), row(GROUP_W), row(GROUP_W), row(GROUP_W), row(GROUP_W),
                  _const_spec((D_MODEL, D_MODEL)), _const_spec((1, D_MODEL)), _const_spec((1, D_MODEL))],
        out_specs=row(D_MODEL),
        out_shape=jax.ShapeDtypeStruct((n, D_MODEL), f32),
        compiler_params=_params("parallel"),
        name="merge",
    )(x2d, ya, yb, yc, yd, wo, g, b)


def _ffn_kernel(alpha, Bt, Lt, F, n_chunks, x_ref, st_ref, win_ref, cw_ref, cb_ref, wout_ref, g_ref, b_ref,
                y_ref, nst_ref, carry_s):
    R = Bt * Lt
    Fc = F // n_chunks

    @pl.when(pl.program_id(1) == 0)
    def _():
        carry_s[...] = st_ref[...]

    if Bt == 1:
        x = x_ref[0]
    else:
        x = jnp.concatenate([x_ref[s] for s in range(Bt)], axis=0)
    xb = x.astype(bf16)
    row_in = _iota((R, 1), 0)
    if Bt > 1:
        assert Lt & (Lt - 1) == 0
        row_in = row_in & (Lt - 1)
    first = row_in == 0
    second = row_in == 1

    def rows_of(prev_row, c0):
        parts = [jnp.broadcast_to(carry_s[s, prev_row:prev_row + 1, c0:c0 + Fc], (Lt, Fc)) for s in range(Bt)]
        return parts[0] if Bt == 1 else jnp.concatenate(parts, axis=0)

    acc = jnp.zeros((R, D_MODEL), f32)
    for c in range(n_chunks):
        c0 = c * Fc
        gt = _dot(xb, win_ref[:, c0:c0 + Fc])
        up = _dot(xb, win_ref[:, F + c0:F + c0 + Fc])
        p0 = rows_of(0, c0)
        p1 = rows_of(1, c0)
        g1 = jnp.where(first, p1, pltpu.roll(gt, 1, 0))
        g2 = jnp.where(first, p0, jnp.where(second, p1, pltpu.roll(gt, 2, 0)))
        conv = (cw_ref[0:1, c0:c0 + Fc] * g2 + cw_ref[1:2, c0:c0 + Fc] * g1
                + cw_ref[2:3, c0:c0 + Fc] * gt + cb_ref[:, c0:c0 + Fc])
        hid = (jax.nn.gelu(conv) * up).astype(bf16)
        acc = acc + _dot(hid, wout_ref[c0:c0 + Fc, :])
        for s in range(Bt):
            carry_s[s, :, c0:c0 + Fc] = gt[s * Lt + Lt - 2:s * Lt + Lt, :]
    y = _layer_norm(alpha * x + acc, g_ref[...], b_ref[...])
    for s in range(Bt):
        y_ref[s] = y[s * Lt:(s + 1) * Lt, :]
    nst_ref[...] = carry_s[...]


def _ffn(alpha, x3d, st, win, cw, cb, wout, g, b):
    B, L, _ = x3d.shape
    F = wout.shape[0]
    if L >= 256:
        Bt, Lt = 1, _row_tile(L, 512, 8)
    else:
        Bt, Lt = B, L
    n_chunks = 2
    blk = lambda shape: pl.BlockSpec((Bt,) + shape, lambda i, t: (i, t) + (0,) * (len(shape) - 1))
    st_spec = pl.BlockSpec((Bt, 2, F), lambda i, t: (i, 0, 0))
    return pl.pallas_call(
        functools.partial(_ffn_kernel, alpha, Bt, Lt, F, n_chunks),
        grid=(B // Bt, L // Lt),
        in_specs=[blk((Lt, D_MODEL)), st_spec, _const_spec((D_MODEL, 2 * F)), _const_spec((3, F)),
                  _const_spec((1, F)), _const_spec((F, D_MODEL)), _const_spec((1, D_MODEL)),
                  _const_spec((1, D_MODEL))],
        out_specs=[blk((Lt, D_MODEL)), st_spec],
        out_shape=[jax.ShapeDtypeStruct((B, L, D_MODEL), f32), jax.ShapeDtypeStruct((B, 2, F), f32)],
        scratch_shapes=[pltpu.VMEM((Bt, 2, F), f32)],
        compiler_params=_params("parallel", "arbitrary"),
        name="ffn",
    )(x3d, st, win, cw, cb, wout, g, b)


def _prep_layer_weights(l, P):
    W = GROUP_W
    w_in, b_in = P['w_in'][l], P['b_in'][l]
    o_fox = 2 * W
    o_ff = o_fox + 3 * W
    o_sb = o_ff + N_HEADS
    o_dn = o_sb + 3 * W
    o_da = o_dn + 3 * W
    o_db = o_da + N_HEADS
    o_dg = o_db + N_HEADS
    pad = SMALL_W - 3 * N_HEADS

    def reorder(a):
        parts = [a[0:o_ff], a[o_sb:o_da], a[o_dg:o_dg + W], a[o_ff:o_ff + N_HEADS],
                 a[o_da:o_da + 2 * N_HEADS], jnp.zeros((pad,) + a.shape[1:], a.dtype)]
        return jnp.concatenate(parts, axis=0)

    w_r = reorder(w_in.T).astype(bf16)
    b_r = reorder(b_in).reshape(1, -1)

    def block_diag(w4):
        eye = jnp.eye(N_HEADS, dtype=w4.dtype)
        return (w4[:, :, None, :] * eye[:, None, :, None]).reshape(W, W)

    small_row = lambda vals, off: jnp.zeros((1, SMALL_W), f32).at[0, off:off + N_HEADS].set(vals)
    r1 = lambda a: a.reshape(1, -1)
    return dict(
        w_in=w_r, b_in=b_r,
        lru_cw=P['lru_conv_w'][l], lru_cb=r1(P['lru_conv_b'][l]),
        lru_wa=block_diag(P['lru_w_a'][l]).astype(bf16), lru_wx=block_diag(P['lru_w_x'][l]).astype(bf16),
        lru_ba=r1(P['lru_b_a'][l]), lru_bx=r1(P['lru_b_x'][l]), lru_lam=r1(P['lru_lambda'][l]),
        gains=[r1(P['grp_norm_g'][l][i]) for i in range(3)],
        dn_cw=P['dn_conv_w'][l], dn_alog=small_row(P['dn_a_log'][l], N_HEADS),
        dn_dtb=small_row(P['dn_dt_bias'][l], N_HEADS), dn_ng=r1(jnp.tile(P['dn_norm_g'][l], N_HEADS)),
        w_o=P['w_o'][l].astype(bf16), ln1_g=r1(P['ln1_g'][l]), ln1_b=r1(P['ln1_b'][l]),
        ffn_win=P['ffn_w_in'][l].astype(bf16), ffn_cw=P['ffn_conv_w'][l], ffn_cb=r1(P['ffn_conv_b'][l]),
        ffn_wout=P['ffn_w_out'][l].astype(bf16), ln2_g=r1(P['ln2_g'][l]), ln2_b=r1(P['ln2_b'][l]),
    )


def _embed_block_diag(S):
    B = S.shape[0]
    eye = jnp.eye(N_HEADS, dtype=S.dtype)
    return (S[:, :, :, None, :] * eye[None, :, None, :, None]).reshape(B, GROUP_W, GROUP_W)


def _extract_block_diag(Sbd):
    return jnp.stack([Sbd[:, h * HEAD_DIM:(h + 1) * HEAD_DIM, h * HEAD_DIM:(h + 1) * HEAD_DIM]
                      for h in range(N_HEADS)], axis=1)


def _trunk(x, st, LW, ln_in, alpha):
    B, L, D = x.shape
    W = GROUP_W
    F = LW[0]['ffn_wout'].shape[0]
    new = {n: [] for n in ('lru_conv', 'lru_h', 'fox_k', 'fox_v', 'fox_logf', 'sb_k', 'sb_v',
                           'dn_conv', 'dn_S', 'ffn_conv')}
    x2 = x.reshape(B * L, D)
    for l, w in enumerate(LW):
        outs = _proj(x2, ln_in[0], ln_in[1], w['w_in'], w['b_in'], do_ln=(l == 0))
        if l == 0:
            x2, outs = outs[0], outs[1:]
        lru, fq, fk, fv, sq, sk, sv, dqkv, dgate, small = [o.reshape(B, L, -1) for o in outs]
        if st is None:
            lru_conv = jnp.zeros((B, 3, W), f32)
            lru_h = jnp.zeros((B, 1, W), f32)
            dn_conv = jnp.zeros((B, 3, 3 * W), f32)
            dn_S = jnp.zeros((B, W, W), f32)
            ffn_conv = jnp.zeros((B, 2, F), f32)
            pfk = pfv = plf = psk = psv = None
        else:
            Pn = st['fox_k'].shape[2]
            lru_conv = st['lru_conv'][l]
            lru_h = st['lru_h'][l].reshape(B, 1, W)
            dn_conv = st['dn_conv'][l]
            dn_S = _embed_block_diag(st['dn_S'][l])
            ffn_conv = st['ffn_conv'][l]
            feat_major = lambda c: jnp.transpose(c.reshape(len(LW), B, Pn, W), (0, 1, 3, 2))
            pfk = feat_major(st['fox_k'])
            pfv = feat_major(st['fox_v'])
            plf = jnp.pad(st['fox_logf'][l], ((0, 0), (0, 0), (0, SMALL_W - N_HEADS)))
            psk = feat_major(st['sb_k'])
            psv = feat_major(st['sb_v'])

        ya, lru_conv_new, h_last = _lru(lru, lru_conv, lru_h, w['lru_cw'], w['lru_cb'], w['lru_wa'],
                                        w['lru_wx'], w['lru_ba'], w['lru_bx'], w['lru_lam'], w['gains'][0])
        kv_t = L >= ATT_SUB
        yb, logf, *fox_kv = _fox(fq, fk, fv, small, pfk, pfv, plf, w['gains'][1], kv_t, l)
        yc, *sb_kv = _sb(sq, sk, sv, psk, psv, w['gains'][2], kv_t, l)
        yd, dn_conv_new, S_new = _dn(dqkv, dgate, small, dn_conv, dn_S, w['dn_cw'], w['dn_alog'],
                                     w['dn_dtb'], w['dn_ng'])
        x1 = _merge(alpha, x2, ya.reshape(B * L, W), yb.reshape(B * L, W), yc.reshape(B * L, W),
                    yd.reshape(B * L, W), w['w_o'], w['ln1_g'], w['ln1_b'])
        x3, ffn_new = _ffn(alpha, x1.reshape(B, L, D), ffn_conv, w['ffn_win'], w['ffn_cw'], w['ffn_cb'],
                           w['ffn_wout'], w['ln2_g'], w['ln2_b'])
        x2 = x3.reshape(B * L, D)

        if kv_t:
            heads = lambda a: jnp.transpose(a.reshape(B, N_HEADS, HEAD_DIM, L), (0, 3, 1, 2))
            kv = fox_kv + sb_kv
        else:
            heads = lambda a: a.reshape(B, L, N_HEADS, HEAD_DIM)
            kv = [fk, fv, sk, sv]
        new['lru_conv'].append(lru_conv_new)
        new['lru_h'].append(h_last.reshape(B, W))
        new['fox_k'].append(heads(kv[0]))
        new['fox_v'].append(heads(kv[1]))
        new['fox_logf'].append(logf[:, :, :N_HEADS])
        new['sb_k'].append(heads(kv[2]))
        new['sb_v'].append(heads(kv[3]))
        new['dn_conv'].append(dn_conv_new)
        new['dn_S'].append(_extract_block_diag(S_new))
        new['ffn_conv'].append(ffn_new)
    return x2.reshape(B, L, D), {n: jnp.stack(v) for n, v in new.items()}


def kernel(x_prompt, x_sample, state_lru_conv, state_lru_h, cache_fox_k, cache_fox_v, cache_fox_logf,
           cache_sb_k, cache_sb_v, state_dn_conv, state_dn_S, state_ffn_conv, meta_tokens, ln_in_g, ln_in_b,
           w_in, b_in, lru_conv_w, lru_conv_b, lru_w_a, lru_b_a, lru_w_x, lru_b_x, lru_lambda, dn_conv_w,
           dn_a_log, dn_dt_bias, dn_norm_g, grp_norm_g, w_o, ln1_g, ln1_b, ffn_w_in, ffn_conv_w, ffn_conv_b,
           ffn_w_out, ln2_g, ln2_b):
    P = dict(w_in=w_in, b_in=b_in, lru_conv_w=lru_conv_w, lru_conv_b=lru_conv_b, lru_w_a=lru_w_a,
             lru_b_a=lru_b_a, lru_w_x=lru_w_x, lru_b_x=lru_b_x, lru_lambda=lru_lambda, dn_conv_w=dn_conv_w,
             dn_a_log=dn_a_log, dn_dt_bias=dn_dt_bias, dn_norm_g=dn_norm_g, grp_norm_g=grp_norm_g, w_o=w_o,
             ln1_g=ln1_g, ln1_b=ln1_b, ffn_w_in=ffn_w_in, ffn_conv_w=ffn_conv_w, ffn_conv_b=ffn_conv_b,
             ffn_w_out=ffn_w_out, ln2_g=ln2_g, ln2_b=ln2_b)
    depth = w_in.shape[0]
    assert x_prompt.shape[2] == D_MODEL
    alpha = (2.0 * depth) ** 0.25
    LW = [_prep_layer_weights(l, P) for l in range(depth)]
    ln_in = (ln_in_g.reshape(1, -1), ln_in_b.reshape(1, -1))

    bp = x_prompt.shape[0]
    meta = jnp.broadcast_to(meta_tokens.astype(x_prompt.dtype), (bp, N_META, D_MODEL))
    xp = jnp.concatenate([meta, x_prompt], axis=1)
    yp, ps = _trunk(xp, None, LW, ln_in, alpha)

    st_in = dict(lru_conv=state_lru_conv, lru_h=state_lru_h, fox_k=cache_fox_k, fox_v=cache_fox_v,
                 fox_logf=cache_fox_logf, sb_k=cache_sb_k, sb_v=cache_sb_v, dn_conv=state_dn_conv,
                 dn_S=state_dn_S, ffn_conv=state_ffn_conv)
    ys, ss = _trunk(x_sample, st_in, LW, ln_in, alpha)

    names = ('lru_conv', 'lru_h', 'fox_k', 'fox_v', 'fox_logf', 'sb_k', 'sb_v', 'dn_conv', 'dn_S', 'ffn_conv')
    return (yp[:, N_META:], ys) + tuple(ps[n] for n in names) + tuple(ss[n] for n in names)
```

```python
import functools

import jax
import jax.numpy as jnp
from jax import lax
from jax.experimental import pallas as pl
from jax.experimental.pallas import tpu as pltpu

f32 = jnp.float32
bf16 = jnp.bfloat16

D_MODEL = 1024
GROUP_W = 256
N_HEADS = 4
HEAD_DIM = 64
N_META = 16
LRU_C = 8.0
LN_EPS = 1e-5
RMS_EPS = 1e-6
ATT_TILE = 256
ATT_SUB = 128
DN_CHUNK = 64
DN_GROUP = 8
NEG = -1e30
LOG2E = 1.4426950408889634
SMALL_W = 128
VMEM_LIMIT_BYTES = 56 * 1024 * 1024

NN = (((1,), (0,)), ((), ()))
NT = (((1,), (1,)), ((), ()))
TN = (((0,), (0,)), ((), ()))


def _dot(a, b, dims=NN):
    return lax.dot_general(a, b, dims, preferred_element_type=f32)


def _split2(x):
    hi = x.astype(bf16)
    lo = (x - hi.astype(f32)).astype(bf16)
    return hi, lo


def _split3(x):
    p0 = x.astype(bf16)
    r = x - p0.astype(f32)
    p1 = r.astype(bf16)
    p2 = (r - p1.astype(f32)).astype(bf16)
    return p0, p1, p2


def _dot_sel_rhs(x, sel, dims=NN):
    p0, p1, p2 = _split3(x)
    return _dot(p0, sel, dims) + _dot(p1, sel, dims) + _dot(p2, sel, dims)


def _dot_sel_lhs(sel, x, dims=NN):
    p0, p1, p2 = _split3(x)
    return _dot(sel, p0, dims) + _dot(sel, p1, dims) + _dot(sel, p2, dims)


def _iota(shape, dim):
    return lax.broadcasted_iota(jnp.int32, shape, dim)


def _head_masks(width=GROUP_W):
    lane_head = _iota((1, width), 1) >> 6
    return [lane_head == h for h in range(N_HEADS)]


def _softplus(x):
    return jnp.maximum(x, 0.0) + jnp.log1p(jnp.exp(-jnp.abs(x)))


def _log_sigmoid(x):
    return jnp.minimum(x, 0.0) - jnp.log1p(jnp.exp(-jnp.abs(x)))


def _layer_norm(x, g, b):
    mu = jnp.mean(x, -1, keepdims=True)
    xc = x - mu
    var = jnp.mean(xc * xc, -1, keepdims=True)
    return xc * lax.rsqrt(var + LN_EPS) * g + b


def _rms_norm(x, g):
    return x * lax.rsqrt(jnp.mean(x * x, -1, keepdims=True) + RMS_EPS) * g


def _row_tile(n, cap, mult=16):
    best = None
    for d in range(mult, min(n, cap) + 1, mult):
        if n % d == 0:
            best = d
    return best if best is not None else n


def _const_spec(shape):
    nd = len(shape)
    return pl.BlockSpec(shape, lambda *_: (0,) * nd, pipeline_mode=pl.Buffered(1))


def _params(*sem):
    return pltpu.CompilerParams(dimension_semantics=sem, vmem_limit_bytes=VMEM_LIMIT_BYTES)


PROJ_WIDTHS = (512, 256, 256, 256, 256, 256, 256, 768, 256, SMALL_W)


def _proj_kernel(do_ln, x_ref, g_ref, b_ref, w_ref, bias_ref, *outs):
    x = x_ref[...]
    if do_ln:
        x = _layer_norm(x, g_ref[...], b_ref[...])
        outs[0][...] = x
        outs = outs[1:]
    xb = x.astype(bf16)
    col = 0
    for o, wd in zip(outs, PROJ_WIDTHS):
        o[...] = _dot(xb, w_ref[col:col + wd, :], NT) + bias_ref[:, col:col + wd]
        col += wd


def _proj(x2d, ln_g, ln_b, w, bias, do_ln):
    n = x2d.shape[0]
    tm = _row_tile(n, 768)
    wtot = w.shape[0]
    row = lambda wd: pl.BlockSpec((tm, wd), lambda i: (i, 0))
    out_shape = [jax.ShapeDtypeStruct((n, wd), f32) for wd in PROJ_WIDTHS]
    out_specs = [row(wd) for wd in PROJ_WIDTHS]
    if do_ln:
        out_shape = [jax.ShapeDtypeStruct((n, D_MODEL), f32)] + out_shape
        out_specs = [row(D_MODEL)] + out_specs
    return pl.pallas_call(
        functools.partial(_proj_kernel, do_ln),
        grid=(n // tm,),
        in_specs=[row(D_MODEL), _const_spec((1, D_MODEL)), _const_spec((1, D_MODEL)),
                  _const_spec((wtot, D_MODEL)), _const_spec((1, wtot))],
        out_specs=out_specs,
        out_shape=out_shape,
        compiler_params=_params("parallel"),
        name="proj",
    )(x2d, ln_g, ln_b, w, bias)


def _lru_kernel(L, R1, lru_ref, cst_ref, h0_ref, cw_ref, cb_ref, wa_ref, wx_ref, ba_ref, bx_ref,
                lam_ref, gain_ref, ya_ref, cnew_ref, hlast_ref, xpad_s, a_s, b_s):
    W = GROUP_W
    xpad_s[pl.ds(0, 8), :] = jnp.zeros((8, W), f32)
    xpad_s[pl.ds(5, 3), :] = cst_ref[0]
    xpad_s[pl.ds(8, L), :] = lru_ref[0, :, 0:W]
    cnew_ref[0] = xpad_s[pl.ds(L + 5, 3), :]

    sp = _softplus(-lam_ref[...])
    cw = cw_ref[...]
    for t in range(L // R1):
        r0 = t * R1
        xa = cb_ref[...]
        for i in range(4):
            xa = xa + cw[i:i + 1, :] * xpad_s[pl.ds(5 + r0 + i, R1), :]
        xb = xa.astype(bf16)
        r = jax.nn.sigmoid(_dot(xb, wa_ref[...]) + ba_ref[...])
        ig = jax.nn.sigmoid(_dot(xb, wx_ref[...]) + bx_ref[...])
        log_a = (-LRU_C) * r * sp
        y2 = 2.0 * log_a
        one_m_a2 = jnp.tanh(-0.5 * y2) * (1.0 + jnp.exp(y2))
        a_s[pl.ds(r0, R1), :] = jnp.exp(log_a)
        b_s[pl.ds(r0, R1), :] = jnp.sqrt(one_m_a2) * (ig * xa)

    rows8 = _iota((8, W), 0)

    def scan_body(g, hprev):
        off = pl.multiple_of(g * 8, 8)
        A = a_s[pl.ds(off, 8), :]
        Bv = b_s[pl.ds(off, 8), :]
        for s in (1, 2, 4):
            keep = rows8 >= s
            a_sh = jnp.where(keep, pltpu.roll(A, s, 0), 1.0)
            b_sh = jnp.where(keep, pltpu.roll(Bv, s, 0), 0.0)
            Bv = A * b_sh + Bv
            A = A * a_sh
        h = A * hprev + Bv
        b_s[pl.ds(off, 8), :] = h
        return h[7:8, :]

    hlast_ref[0] = lax.fori_loop(0, L // 8, scan_body, h0_ref[0])

    for t in range(L // R1):
        r0 = t * R1
        o = jax.nn.gelu(lru_ref[0, pl.ds(r0, R1), W:2 * W]) * b_s[pl.ds(r0, R1), :]
        ya_ref[0, pl.ds(r0, R1), :] = _rms_norm(o, gain_ref[...]).astype(bf16)


def _lru(lru3d, conv_st, h0, cw, cb, wa, wx, ba, bx, lam, gain):
    B, L, _ = lru3d.shape
    R1 = _row_tile(L, 768)
    W = GROUP_W
    per_b = lambda shape: pl.BlockSpec((1,) + shape, lambda b: (b,) + (0,) * len(shape))
    return pl.pallas_call(
        functools.partial(_lru_kernel, L, R1),
        grid=(B,),
        in_specs=[per_b((L, 2 * W)), per_b((3, W)), per_b((1, W)),
                  _const_spec((4, W)), _const_spec((1, W)), _const_spec((W, W)), _const_spec((W, W)),
                  _const_spec((1, W)), _const_spec((1, W)), _const_spec((1, W)), _const_spec((1, W))],
        out_specs=[per_b((L, W)), per_b((3, W)), per_b((1, W))],
        out_shape=[jax.ShapeDtypeStruct((B, L, W), bf16), jax.ShapeDtypeStruct((B, 3, W), f32),
                   jax.ShapeDtypeStruct((B, 1, W), f32)],
        scratch_shapes=[pltpu.VMEM((L + 8, W), f32), pltpu.VMEM((L, W), f32), pltpu.VMEM((L, W), f32)],
        compiler_params=_params("parallel"),
        name="lru",
    )(lru3d, conv_st, h0, cw, cb, wa, wx, ba, bx, lam, gain)


def _att_layout(L, P):
    T = ATT_TILE
    assert (P + L) % 16 == 0
    front = (-(P + L)) % T
    Lpad = front + P + L
    L0 = L % T
    nT = L // T
    base_t = (front + P + L0) // T
    return front, Lpad, L0, nT, base_t


def _stage_pairs(front, P, L, new_ref, past_ref, flat_s, pair_s, hm):
    W = GROUP_W
    S = ATT_SUB
    if front:
        flat_s[pl.ds(0, front), :] = jnp.zeros((front, W), bf16)
    if P:
        assert P % S == 0

        def past_body(u, c):
            r = pl.multiple_of(u * S, S)
            flat_s[pl.ds(pl.multiple_of(front + r, 16), S), :] = past_ref[0, :, pl.ds(r, S)].T.astype(bf16)
            return c

        lax.fori_loop(0, P // S, past_body, 0)
    flat_s[pl.ds(front + P, L), :] = new_ref[0].astype(bf16)
    zero = jnp.zeros((S, W), bf16)

    def body(u, c):
        x = flat_s[pl.ds(pl.multiple_of(u * S, S), S), :]
        base = pl.multiple_of(u * 2 * S, 2 * S)
        for p in range(2):
            pair_s[p, pl.ds(base, S), :] = jnp.where(hm[2 * p], x, zero)
            pair_s[p, pl.ds(base + S, S), :] = jnp.where(hm[2 * p + 1], x, zero)
        return c

    lax.fori_loop(0, (front + P + L) // S, body, 0)


def _store_transposed(src_ref, dst_ref, L):
    S = ATT_SUB
    n_full = L // S

    def body(u, c):
        r = pl.multiple_of(u * S, S)
        dst_ref[0, :, pl.ds(r, S)] = src_ref[0, pl.ds(r, S), :].T
        return c

    lax.fori_loop(0, n_full, body, 0)
    tail = L - n_full * S
    if tail:
        eye = (_iota((GROUP_W, GROUP_W), 0) == _iota((GROUP_W, GROUP_W), 1)).astype(bf16)
        dst_ref[0, :, pl.ds(n_full * S, tail)] = _dot_sel_lhs(eye, src_ref[0, pl.ds(n_full * S, tail), :], NT)


def _pair_scores(qb, kpair_s, j):
    S2 = 2 * ATT_SUB
    kbase = j * (2 * ATT_TILE)
    return [[_dot(qb, kpair_s[p, pl.ds(pl.multiple_of(kbase + sub * S2, S2), S2), :], NT)
             for sub in range(2)] for p in range(2)]


def _pair_pv(wb, vpair_s, j):
    S2 = 2 * ATT_SUB
    kbase = j * (2 * ATT_TILE)
    out = None
    for sub in range(2):
        for p in range(2):
            lhs = jnp.concatenate([wb[2 * p][sub], wb[2 * p + 1][sub]], axis=1)
            t = _dot(lhs, vpair_s[p, pl.ds(pl.multiple_of(kbase + sub * S2, S2), S2), :])
            out = t if out is None else out + t
    return out


def _by_head_lanes(vals, lo_half):
    return jnp.concatenate([jnp.where(lo_half, vals[0], vals[1]), jnp.where(lo_half, vals[2], vals[3])], axis=1)


def _fox_kernel(L, P, emit_t, *refs):
    T, S = ATT_TILE, ATT_SUB
    front, Lpad, L0, nT, base_t = _att_layout(L, P)
    n_in = 8 if P else 5
    n_out = 4 if emit_t else 2
    if P:
        q_ref, k_ref, v_ref, sm_ref, pk_ref, pv_ref, plf_ref, gain_ref = refs[:n_in]
    else:
        q_ref, k_ref, v_ref, sm_ref, gain_ref = refs[:n_in]
        pk_ref = pv_ref = plf_ref = None
    yb_ref, lf_ref = refs[n_in:n_in + 2]
    scr = refs[n_in + n_out:]
    flat_s, kpair_s, vpair_s, lfp_s, F_s, FT_s, qb_s, fqb_s, m_s, lp_s, acc_s, cb_s = scr
    hm = _head_masks()
    lo_half = _iota((1, S), 1) < HEAD_DIM
    scale = HEAD_DIM ** -0.5 * LOG2E
    if emit_t:
        _store_transposed(k_ref, refs[n_in + 2], L)
        _store_transposed(v_ref, refs[n_in + 3], L)

    _stage_pairs(front, P, L, k_ref, pk_ref, flat_s, kpair_s, hm)
    _stage_pairs(front, P, L, v_ref, pv_ref, flat_s, vpair_s, hm)

    if front:
        lfp_s[pl.ds(0, front), :] = jnp.zeros((front, SMALL_W), f32)
    if P:
        lfp_s[pl.ds(front, P), :] = plf_ref[0]
    lf = _log_sigmoid(sm_ref[0])
    lf_ref[0] = lf
    lfp_s[pl.ds(front + P, L), :] = lf
    ltri = (_iota((S, S), 0) >= _iota((S, S), 1)).astype(bf16)
    sel8 = (_iota((8, SMALL_W), 0) == _iota((8, SMALL_W), 1)).astype(bf16)

    n_ct = Lpad // S
    local = [_dot_sel_lhs(ltri, lfp_s[pl.ds(u * S, S), :]) for u in range(n_ct)]
    offset = jnp.zeros((1, SMALL_W), f32)
    F2 = []
    for u in range(n_ct):
        F2.append((local[u] + offset) * LOG2E)
        offset = offset + local[u][S - 1:S, :]
    for u in range(n_ct):
        F_s[pl.ds(u * S, S), :] = F2[u]
    for u in range(n_ct):
        FT_s[:, pl.ds(u * S, S)] = _dot_sel_lhs(sel8, F2[u], NT)
    if front:
        FT_s[:, pl.ds(0, front)] = jnp.full((8, front), -NEG, f32)
    cb_s[...] = jnp.where(_iota((T, T), 1) <= _iota((T, T), 0), 0.0, NEG)

    def process(qnat, tq, d):
        rows = pl.ds(0, tq)
        qp0 = front + P + qnat
        qb_s[rows, :] = (q_ref[0, pl.ds(qnat, tq), :] * scale).astype(bf16)
        Fq = F_s[pl.ds(qp0, tq), :]
        for h in range(N_HEADS):
            fqb_s[h, rows, :] = jnp.broadcast_to(Fq[:, h:h + 1], (tq, S))
            m_s[h, rows, :] = jnp.full((tq, S), NEG, f32)
            lp_s[h, rows, :] = jnp.zeros((tq, S), f32)
        acc_s[rows, :] = jnp.zeros((tq, GROUP_W), f32)

        def scores(j):
            return _pair_scores(qb_s[rows, :], kpair_s, j)

        def update(j, sc, mode):
            if mode == 'ragged':
                qpos = qp0 + _iota((tq, S), 0)
                vis = []
                for sub in range(2):
                    kpos = j * T + sub * S + _iota((tq, S), 1)
                    ok = kpos <= qpos
                    if front:
                        ok = ok & (kpos >= front)
                    vis.append(ok)
            alphas, pb = [], []
            for h in range(N_HEADS):
                p_, half = divmod(h, 2)
                fq = fqb_s[h, rows, :]
                c = []
                for sub in range(2):
                    fk = FT_s[h:h + 1, pl.ds(pl.multiple_of(j * T + sub * S, S), S)]
                    x = sc[p_][sub][:, half * S:(half + 1) * S] + (fq - fk)
                    if mode == 'ragged':
                        x = jnp.where(vis[sub], x, NEG)
                    elif mode == 'diag':
                        x = x + cb_s[:, sub * S:(sub + 1) * S]
                    c.append(x)
                m_old = m_s[h, rows, :]
                m_new = jnp.maximum(m_old, jnp.max(jnp.maximum(c[0], c[1]), -1, keepdims=True))
                alpha = jnp.exp2(m_old - m_new)
                e0 = jnp.exp2(c[0] - m_new)
                e1 = jnp.exp2(c[1] - m_new)
                lp_s[h, rows, :] = alpha * lp_s[h, rows, :] + (e0 + e1)
                m_s[h, rows, :] = m_new
                alphas.append(alpha)
                pb.append((e0.astype(bf16), e1.astype(bf16)))
            pv = _pair_pv(pb, vpair_s, j)
            acc_s[rows, :] = acc_s[rows, :] * _by_head_lanes(alphas, lo_half) + pv

        def pair(j0, mode0, j1, mode1):
            sc0 = scores(j0)
            sc1 = scores(j1)
            update(j0, sc0, mode0)
            update(j1, sc1, mode1)

        def plain2(t, c):
            pair(2 * t, 'plain', 2 * t + 1, 'plain')
            return c

        lax.fori_loop(0, d // 2, plain2, 0)
        if isinstance(d, int):
            if d % 2:
                pair(d - 1, 'plain', d, 'ragged')
            else:
                update(d, scores(d), 'ragged')
        else:
            @pl.when(d % 2 == 1)
            def _():
                pair(d - 1, 'plain', d, 'diag')

            @pl.when(d % 2 == 0)
            def _():
                update(d, scores(d), 'diag')

        l = [jnp.sum(lp_s[h, rows, :], -1, keepdims=True) for h in range(N_HEADS)]
        out = acc_s[rows, :] / _by_head_lanes(l, lo_half)
        yb_ref[0, pl.ds(qnat, tq), :] = _rms_norm(out, gain_ref[...]).astype(bf16)

    if L0:
        process(0, L0, base_t - 1)

    def q_body(i, c):
        process(pl.multiple_of(L0 + i * T, 16), T, base_t + i)
        return c

    lax.fori_loop(0, nT, q_body, 0)


def _fox(q, k, v, small, past_k, past_v, past_lf, gain, emit_t, layer):
    B, L, W = q.shape
    P = 0 if past_k is None else past_k.shape[3]
    T = ATT_TILE
    _, Lpad, _, _, _ = _att_layout(L, P)
    per_b = lambda shape: pl.BlockSpec((1,) + shape, lambda b: (b,) + (0,) * len(shape))
    per_lb = lambda shape: pl.BlockSpec((None, 1) + shape, lambda b: (layer, b) + (0,) * len(shape))
    in_specs = [per_b((L, W)), per_b((L, W)), per_b((L, W)), per_b((L, SMALL_W))]
    args = [q, k, v, small]
    if P:
        in_specs += [per_lb((W, P)), per_lb((W, P)), per_b((P, SMALL_W))]
        args += [past_k, past_v, past_lf]
    in_specs.append(_const_spec((1, W)))
    args.append(gain)
    out_specs = [per_b((L, W)), per_b((L, SMALL_W))]
    out_shape = [jax.ShapeDtypeStruct((B, L, W), bf16), jax.ShapeDtypeStruct((B, L, SMALL_W), f32)]
    if emit_t:
        out_specs += [per_b((W, L)), per_b((W, L))]
        out_shape += [jax.ShapeDtypeStruct((B, W, L), f32)] * 2
    return pl.pallas_call(
        functools.partial(_fox_kernel, L, P, emit_t),
        grid=(B,),
        in_specs=in_specs,
        out_specs=out_specs,
        out_shape=out_shape,
        scratch_shapes=[pltpu.VMEM((Lpad, W), bf16), pltpu.VMEM((2, 2 * Lpad, W), bf16),
                        pltpu.VMEM((2, 2 * Lpad, W), bf16), pltpu.VMEM((Lpad, SMALL_W), f32),
                        pltpu.VMEM((Lpad, SMALL_W), f32), pltpu.VMEM((8, Lpad), f32),
                        pltpu.VMEM((T, W), bf16), pltpu.VMEM((N_HEADS, T, ATT_SUB), f32),
                        pltpu.VMEM((N_HEADS, T, ATT_SUB), f32), pltpu.VMEM((N_HEADS, T, ATT_SUB), f32),
                        pltpu.VMEM((T, W), f32), pltpu.VMEM((T, T), f32)],
        compiler_params=_params("parallel"),
        name="fox",
    )(*args)


def _sb_kernel(L, P, emit_t, *refs):
    T, S = ATT_TILE, ATT_SUB
    front, Lpad, L0, nT, base_t = _att_layout(L, P)
    n_in = 6 if P else 4
    n_out = 3 if emit_t else 1
    if P:
        q_ref, k_ref, v_ref, pk_ref, pv_ref, gain_ref = refs[:n_in]
    else:
        q_ref, k_ref, v_ref, gain_ref = refs[:n_in]
        pk_ref = pv_ref = None
    yc_ref = refs[n_in]
    scr = refs[n_in + n_out:]
    flat_s, kpair_s, vpair_s, qb_s, c_s, acc_s, cb_s = scr
    hm = _head_masks()
    scale = HEAD_DIM ** -0.5 * LOG2E
    if emit_t:
        _store_transposed(k_ref, refs[n_in + 1], L)
        _store_transposed(v_ref, refs[n_in + 2], L)

    _stage_pairs(front, P, L, k_ref, pk_ref, flat_s, kpair_s, hm)
    _stage_pairs(front, P, L, v_ref, pv_ref, flat_s, vpair_s, hm)

    after2 = ((_iota((2 * T, T), 0) & (T - 1)) >= _iota((2 * T, T), 1)).astype(bf16)
    cb_s[...] = jnp.where(_iota((T, T), 1) < _iota((T, T), 0), 0.0, NEG)
    pad_bias = jnp.where(_iota((1, T), 1) >= front, 0.0, NEG)

    def process(qnat, tq, d):
        rows = pl.ds(0, tq)
        qp0 = front + P + qnat
        qb_s[rows, :] = (q_ref[0, pl.ds(qnat, tq), :] * scale).astype(bf16)
        for h in range(N_HEADS):
            c_s[h, rows, :] = jnp.zeros((tq, S), f32)
        acc_s[rows, :] = jnp.zeros((tq, GROUP_W), f32)

        def scores(j):
            return _pair_scores(qb_s[rows, :], kpair_s, j)

        def update(j, sc, mode):
            if mode == 'ragged':
                kpos = j * T + _iota((tq, T), 1)
                vis = kpos < qp0 + _iota((tq, T), 0)
                if front:
                    vis = vis & (kpos >= front)
            wb = []
            for h in range(N_HEADS):
                p_, half = divmod(h, 2)
                z = jnp.concatenate([sc[p_][sub][:, half * S:(half + 1) * S] for sub in range(2)], axis=1)
                if mode == 'diag':
                    z = z + cb_s[...]
                elif mode == 'pad':
                    z = z + pad_bias
                nk = jnp.maximum(z, 0.0) + jnp.log2(1.0 + jnp.exp2(-jnp.abs(z)))
                if mode == 'ragged':
                    nk = jnp.where(vis, nk, 0.0)
                hi, lo = _split2(nk)
                incl = _dot(jnp.concatenate([hi, lo], axis=1), after2)
                c = c_s[h, rows, :]
                w = jnp.exp2(z - incl - jnp.concatenate([c, c], axis=1))
                if mode == 'ragged':
                    w = jnp.where(vis, w, 0.0)
                c_s[h, rows, :] = c + jnp.sum(nk, -1, keepdims=True)
                wbf = w.astype(bf16)
                wb.append((wbf[:, 0:S], wbf[:, S:2 * S]))
            acc_s[rows, :] = acc_s[rows, :] + _pair_pv(wb, vpair_s, j)

        def kv_step(j, mode):
            update(j, scores(j), mode)

        def plain2(t, c):
            j = d - 1 - 2 * t
            sc0 = scores(j)
            sc1 = scores(j - 1)
            update(j, sc0, 'plain')
            update(j - 1, sc1, 'plain')
            return c

        jf = 1 if front else 0
        kv_step(d, 'ragged' if isinstance(d, int) else 'diag')
        if (not isinstance(d, int)) or d > 0:
            n_plain = d - jf
            lax.fori_loop(0, n_plain // 2, plain2, 0)

            @pl.when(n_plain % 2 == 1)
            def _():
                kv_step(jf, 'plain')
            if front:
                kv_step(0, 'pad')

        yc_ref[0, pl.ds(qnat, tq), :] = _rms_norm(acc_s[rows, :], gain_ref[...]).astype(bf16)

    if L0:
        process(0, L0, base_t - 1)

    def q_body(i, c):
        process(pl.multiple_of(L0 + i * T, 16), T, base_t + i)
        return c

    lax.fori_loop(0, nT, q_body, 0)


def _sb(q, k, v, past_k, past_v, gain, emit_t, layer):
    B, L, W = q.shape
    P = 0 if past_k is None else past_k.shape[3]
    per_lb = lambda shape: pl.BlockSpec((None, 1) + shape, lambda b: (layer, b) + (0,) * len(shape))
    T = ATT_TILE
    _, Lpad, _, _, _ = _att_layout(L, P)
    per_b = lambda shape: pl.BlockSpec((1,) + shape, lambda b: (b,) + (0,) * len(shape))
    in_specs = [per_b((L, W)), per_b((L, W)), per_b((L, W))]
    args = [q, k, v]
    if P:
        in_specs += [per_lb((W, P)), per_lb((W, P))]
        args += [past_k, past_v]
    in_specs.append(_const_spec((1, W)))
    args.append(gain)
    out_specs = [per_b((L, W))]
    out_shape = [jax.ShapeDtypeStruct((B, L, W), bf16)]
    if emit_t:
        out_specs += [per_b((W, L)), per_b((W, L))]
        out_shape += [jax.ShapeDtypeStruct((B, W, L), f32)] * 2
    return pl.pallas_call(
        functools.partial(_sb_kernel, L, P, emit_t),
        grid=(B,),
        in_specs=in_specs,
        out_specs=out_specs,
        out_shape=out_shape,
        scratch_shapes=[pltpu.VMEM((Lpad, W), bf16), pltpu.VMEM((2, 2 * Lpad, W), bf16),
                        pltpu.VMEM((2, 2 * Lpad, W), bf16), pltpu.VMEM((T, W), bf16),
                        pltpu.VMEM((N_HEADS, T, ATT_SUB), f32), pltpu.VMEM((T, W), f32),
                        pltpu.VMEM((T, T), f32)],
        compiler_params=_params("parallel"),
        name="sb",
    )(*args)


def _dn_kernel(L, R1, qkv_ref, gate_ref, sm_ref, cst_ref, s0_ref, cw_ref, alog_ref, dtb_ref, ng_ref,
               yd_ref, cnew_ref, sout_ref, xpad_s, q_s, k_s, v_s, g_s, beta_s, o_s, S_s):
    W = GROUP_W
    hm = _head_masks()
    bd_ones = ((_iota((W, W), 0) >> 6) == (_iota((W, W), 1) >> 6)).astype(bf16)
    exp_g = (_iota((SMALL_W, W), 0) == (_iota((SMALL_W, W), 1) >> 6) + N_HEADS).astype(bf16)
    exp_b = (_iota((SMALL_W, W), 0) == (_iota((SMALL_W, W), 1) >> 6) + 2 * N_HEADS).astype(bf16)

    def head_sum(x):
        hi, lo = _split2(x)
        return _dot(hi, bd_ones) + _dot(lo, bd_ones)

    xpad_s[pl.ds(0, 8), :] = jnp.zeros((8, 3 * W), f32)
    xpad_s[pl.ds(5, 3), :] = cst_ref[0]
    xpad_s[pl.ds(8, L), :] = qkv_ref[0]
    cnew_ref[0] = xpad_s[pl.ds(L + 5, 3), :]
    cw = cw_ref[...]
    neg_a = -jnp.exp(alog_ref[...])
    for t in range(L // R1):
        r0 = t * R1
        rows = pl.ds(r0, R1)
        xc = cw[0:1, :] * xpad_s[pl.ds(5 + r0, R1), :]
        for i in range(1, 4):
            xc = xc + cw[i:i + 1, :] * xpad_s[pl.ds(5 + r0 + i, R1), :]
        xc = xc * jax.nn.sigmoid(xc)
        q = xc[:, 0:W]
        k = xc[:, W:2 * W]
        q_s[rows, :] = q * lax.rsqrt(head_sum(q * q) + RMS_EPS) * (HEAD_DIM ** -0.5)
        k_s[rows, :] = k * lax.rsqrt(head_sum(k * k) + RMS_EPS)
        v_s[rows, :] = xc[:, 2 * W:3 * W]
        sm = sm_ref[0, rows, :]
        g_s[rows, :] = _dot_sel_rhs(neg_a * _softplus(sm + dtb_ref[...]), exp_g)
        beta_s[rows, :] = _dot_sel_rhs(jax.nn.sigmoid(sm), exp_b)

    S_s[...] = s0_ref[0]
    bd_mask = (_iota((W, W), 0) >> 6) == (_iota((W, W), 1) >> 6)

    def part_a(offs, C):
        each = lambda f, *ls: [f(*a) for a in zip(*ls)]
        CW = N_HEADS * C
        assert C & (C - 1) == 0
        ri = _iota((C, CW), 0)
        si = _iota((C, CW), 1) & (C - 1)
        blk = [(_iota((1, CW), 1) >> (C.bit_length() - 1)) == h for h in range(N_HEADS)]
        ltri = (_iota((C, C), 0) >= _iota((C, C), 1)).astype(bf16)

        def head_rows(x):
            xb = x.astype(bf16)
            return jnp.concatenate([jnp.where(hm[h], xb, jnp.zeros_like(xb)) for h in range(N_HEADS)], axis=0)

        def head_blocks(y):
            yb = y.astype(bf16)
            return jnp.concatenate([jnp.where(blk[h], yb, jnp.zeros_like(yb)) for h in range(N_HEADS)], axis=0)

        q = [q_s[pl.ds(off, C), :] for off in offs]
        k = [k_s[pl.ds(off, C), :] for off in offs]
        beta = [beta_s[pl.ds(off, C), :] for off in offs]
        gc = [_dot_sel_lhs(ltri, g_s[pl.ds(off, C), :]) for off in offs]
        eg = each(jnp.exp, gc)
        g_last = [g[C - 1:C, :] for g in gc]
        kb = each(lambda a, b: a * b, k, beta)
        vb = [v_s[pl.ds(off, C), :] * b for off, b in zip(offs, beta)]
        kbe = each(lambda a, b: a * b, kb, eg)
        yield

        gparts = each(_split3, gc)
        first_lane = ((_iota((C, W), 1) & (HEAD_DIM - 1)) == 0).astype(bf16)
        first3 = jnp.concatenate([first_lane] * 3, axis=1)
        g_row = [_dot(first3, jnp.concatenate([head_rows(p) for p in gp], axis=1), NT) for gp in gparts]
        if CW == W:
            g_col = gc
        else:
            spread = (_iota((W, CW), 0) == (_iota((W, CW), 1) >> (C.bit_length() - 1)) * HEAD_DIM).astype(bf16)
            spread3 = jnp.concatenate([spread] * 3, axis=0)
            g_col = [_dot(jnp.concatenate(gp, axis=1), spread3) for gp in gparts]
        yield
        decay = each(lambda c, r: jnp.exp(jnp.minimum(c - r, 0.0)), g_col, g_row)
        k_rows = each(head_rows, k)
        m = each(lambda a, kr, d: jnp.where(ri > si, _dot(a.astype(bf16), kr, NT) * d, 0.0), kb, k_rows, decay)
        a_in = each(lambda a, kr, d: jnp.where(ri >= si, _dot(a.astype(bf16), kr, NT) * d, 0.0).astype(bf16),
                    q, k_rows, decay)
        yield
        pw = [-a for a in m]
        x = pw
        pw_bd = each(head_blocks, pw)
        for _ in range((C - 1).bit_length() - 1):
            pw = each(lambda a, b: _dot(a.astype(bf16), b), pw, pw_bd)
            pw_bd = each(head_blocks, pw)
            x = each(lambda a, p, b: a + p + _dot(a.astype(bf16), b), x, pw, pw_bd)
            yield
        xb = [a.astype(bf16) for a in x]
        u = each(lambda a, b: b + _dot(a, head_rows(b)), xb, vb)
        w = each(lambda a, b: (b + _dot(a, head_rows(b))).astype(bf16), xb, kbe)
        yield
        qe = each(lambda a, b: (a * b).astype(bf16), q, eg)
        kd = each(lambda a, gl, g: (a * jnp.exp(gl - g)).astype(bf16), k, g_last, gc)
        return list(zip(u, w, a_in, qe, kd, each(jnp.exp, g_last)))

    def finish(gen):
        try:
            while True:
                next(gen)
        except StopIteration as stop:
            return stop.value

    def part_b(off, C, u, wb, a_in, qe, kd, eg_last):
        S = S_s[...]
        Sb = S.astype(bf16)
        v_new = u - _dot(wb, Sb)
        vnb = v_new.astype(bf16)
        v_rows = jnp.concatenate([jnp.where(hm[h], vnb, jnp.zeros_like(vnb)) for h in range(N_HEADS)], axis=0)
        o_s[pl.ds(off, C), :] = _dot(qe, Sb) + _dot(a_in, v_rows)
        S_s[...] = S * eg_last + jnp.where(bd_mask, _dot(kd, vnb, TN), 0.0)

    C0 = L % DN_CHUNK
    if C0:
        part_b(0, C0, *finish(part_a([0], C0))[0])

    n_blocks = L // DN_CHUNK
    group = max(g for g in (DN_GROUP, 4, 2, 1) if n_blocks % g == 0)

    def group_body(c, carry):
        offs = [pl.multiple_of(C0 + (c * group + i) * DN_CHUNK, 16) for i in range(group)]
        for off, a in zip(offs, finish(part_a(offs, DN_CHUNK))):
            part_b(off, DN_CHUNK, *a)
        return carry

    lax.fori_loop(0, n_blocks // group, group_body, 0)
    sout_ref[0] = S_s[...]

    for t in range(L // R1):
        rows = pl.ds(t * R1, R1)
        o = o_s[rows, :]
        gate = gate_ref[0, rows, :]
        y = o * lax.rsqrt(head_sum(o * o) * (1.0 / HEAD_DIM) + RMS_EPS) * ng_ref[...]
        yd_ref[0, rows, :] = (y * (gate * jax.nn.sigmoid(gate))).astype(bf16)


def _dn(qkv, gate, small, conv_st, s0_bd, cw, alog_row, dtb_row, ng_row):
    B, L, _ = qkv.shape
    W = GROUP_W
    R1 = _row_tile(L, 768)
    per_b = lambda shape: pl.BlockSpec((1,) + shape, lambda b: (b,) + (0,) * len(shape))
    return pl.pallas_call(
        functools.partial(_dn_kernel, L, R1),
        grid=(B,),
        in_specs=[per_b((L, 3 * W)), per_b((L, W)), per_b((L, SMALL_W)), per_b((3, 3 * W)), per_b((W, W)),
                  _const_spec((4, 3 * W)), _const_spec((1, SMALL_W)), _const_spec((1, SMALL_W)),
                  _const_spec((1, W))],
        out_specs=[per_b((L, W)), per_b((3, 3 * W)), per_b((W, W))],
        out_shape=[jax.ShapeDtypeStruct((B, L, W), bf16), jax.ShapeDtypeStruct((B, 3, 3 * W), f32),
                   jax.ShapeDtypeStruct((B, W, W), f32)],
        scratch_shapes=[pltpu.VMEM((L + 8, 3 * W), f32)] + [pltpu.VMEM((L, W), f32)] * 6
                       + [pltpu.VMEM((W, W), f32)],
        compiler_params=_params("parallel"),
        name="dn",
    )(qkv, gate, small, conv_st, s0_bd, cw, alog_row, dtb_row, ng_row)


def _merge_kernel(alpha, x_ref, ya_ref, yb_ref, yc_ref, yd_ref, wo_ref, g_ref, b_ref, o_ref):
    W = GROUP_W
    m = _dot(ya_ref[...], wo_ref[0:W, :])
    m = m + _dot(yb_ref[...], wo_ref[W:2 * W, :])
    m = m + _dot(yc_ref[...], wo_ref[2 * W:3 * W, :])
    m = m + _dot(yd_ref[...], wo_ref[3 * W:4 * W, :])
    o_ref[...] = _layer_norm(alpha * x_ref[...] + m, g_ref[...], b_ref[...])


def _merge(alpha, x2d, ya, yb, yc, yd, wo, g, b):
    n = x2d.shape[0]
    tm = _row_tile(n, 768)
    row = lambda wd: pl.BlockSpec((tm, wd), lambda i: (i, 0))
    return pl.pallas_call(
        functools.partial(_merge_kernel, alpha),
        grid=(n // tm,),
        in_specs=[row(D_MODEL), row(GROUP_W), row(GROUP_W), row(GROUP_W), row(GROUP_W),
                  _const_spec((D_MODEL, D_MODEL)), _const_spec((1, D_MODEL)), _const_spec((1, D_MODEL))],
        out_specs=row(D_MODEL),
        out_shape=jax.ShapeDtypeStruct((n, D_MODEL), f32),
        compiler_params=_params("parallel"),
        name="merge",
    )(x2d, ya, yb, yc, yd, wo, g, b)


def _ffn_kernel(alpha, Bt, Lt, F, n_chunks, x_ref, st_ref, win_ref, cw_ref, cb_ref, wout_ref, g_ref, b_ref,
                y_ref, nst_ref, carry_s):
    R = Bt * Lt
    Fc = F // n_chunks

    @pl.when(pl.program_id(1) == 0)
    def _():
        carry_s[...] = st_ref[...]

    if Bt == 1:
        x = x_ref[0]
    else:
        x = jnp.concatenate([x_ref[s] for s in range(Bt)], axis=0)
    xb = x.astype(bf16)
    row_in = _iota((R, 1), 0)
    if Bt > 1:
        assert Lt & (Lt - 1) == 0
        row_in = row_in & (Lt - 1)
    first = row_in == 0
    second = row_in == 1

    def rows_of(prev_row, c0):
        parts = [jnp.broadcast_to(carry_s[s, prev_row:prev_row + 1, c0:c0 + Fc], (Lt, Fc)) for s in range(Bt)]
        return parts[0] if Bt == 1 else jnp.concatenate(parts, axis=0)

    acc = jnp.zeros((R, D_MODEL), f32)
    for c in range(n_chunks):
        c0 = c * Fc
        gt = _dot(xb, win_ref[:, c0:c0 + Fc])
        up = _dot(xb, win_ref[:, F + c0:F + c0 + Fc])
        p0 = rows_of(0, c0)
        p1 = rows_of(1, c0)
        g1 = jnp.where(first, p1, pltpu.roll(gt, 1, 0))
        g2 = jnp.where(first, p0, jnp.where(second, p1, pltpu.roll(gt, 2, 0)))
        conv = (cw_ref[0:1, c0:c0 + Fc] * g2 + cw_ref[1:2, c0:c0 + Fc] * g1
                + cw_ref[2:3, c0:c0 + Fc] * gt + cb_ref[:, c0:c0 + Fc])
        hid = (jax.nn.gelu(conv) * up).astype(bf16)
        acc = acc + _dot(hid, wout_ref[c0:c0 + Fc, :])
        for s in range(Bt):
            carry_s[s, :, c0:c0 + Fc] = gt[s * Lt + Lt - 2:s * Lt + Lt, :]
    y = _layer_norm(alpha * x + acc, g_ref[...], b_ref[...])
    for s in range(Bt):
        y_ref[s] = y[s * Lt:(s + 1) * Lt, :]
    nst_ref[...] = carry_s[...]


def _ffn(alpha, x3d, st, win, cw, cb, wout, g, b):
    B, L, _ = x3d.shape
    F = wout.shape[0]
    if L >= 256:
        Bt, Lt = 1, _row_tile(L, 512, 8)
    else:
        Bt, Lt = B, L
    n_chunks = 2
    blk = lambda shape: pl.BlockSpec((Bt,) + shape, lambda i, t: (i, t) + (0,) * (len(shape) - 1))
    st_spec = pl.BlockSpec((Bt, 2, F), lambda i, t: (i, 0, 0))
    return pl.pallas_call(
        functools.partial(_ffn_kernel, alpha, Bt, Lt, F, n_chunks),
        grid=(B // Bt, L // Lt),
        in_specs=[blk((Lt, D_MODEL)), st_spec, _const_spec((D_MODEL, 2 * F)), _const_spec((3, F)),
                  _const_spec((1, F)), _const_spec((F, D_MODEL)), _const_spec((1, D_MODEL)),
                  _const_spec((1, D_MODEL))],
        out_specs=[blk((Lt, D_MODEL)), st_spec],
        out_shape=[jax.ShapeDtypeStruct((B, L, D_MODEL), f32), jax.ShapeDtypeStruct((B, 2, F), f32)],
        scratch_shapes=[pltpu.VMEM((Bt, 2, F), f32)],
        compiler_params=_params("parallel", "arbitrary"),
        name="ffn",
    )(x3d, st, win, cw, cb, wout, g, b)


def _prep_layer_weights(l, P):
    W = GROUP_W
    w_in, b_in = P['w_in'][l], P['b_in'][l]
    o_fox = 2 * W
    o_ff = o_fox + 3 * W
    o_sb = o_ff + N_HEADS
    o_dn = o_sb + 3 * W
    o_da = o_dn + 3 * W
    o_db = o_da + N_HEADS
    o_dg = o_db + N_HEADS
    pad = SMALL_W - 3 * N_HEADS

    def reorder(a):
        parts = [a[0:o_ff], a[o_sb:o_da], a[o_dg:o_dg + W], a[o_ff:o_ff + N_HEADS],
                 a[o_da:o_da + 2 * N_HEADS], jnp.zeros((pad,) + a.shape[1:], a.dtype)]
        return jnp.concatenate(parts, axis=0)

    w_r = reorder(w_in.T).astype(bf16)
    b_r = reorder(b_in).reshape(1, -1)

    def block_diag(w4):
        eye = jnp.eye(N_HEADS, dtype=w4.dtype)
        return (w4[:, :, None, :] * eye[:, None, :, None]).reshape(W, W)

    small_row = lambda vals, off: jnp.zeros((1, SMALL_W), f32).at[0, off:off + N_HEADS].set(vals)
    r1 = lambda a: a.reshape(1, -1)
    return dict(
        w_in=w_r, b_in=b_r,
        lru_cw=P['lru_conv_w'][l], lru_cb=r1(P['lru_conv_b'][l]),
        lru_wa=block_diag(P['lru_w_a'][l]).astype(bf16), lru_wx=block_diag(P['lru_w_x'][l]).astype(bf16),
        lru_ba=r1(P['lru_b_a'][l]), lru_bx=r1(P['lru_b_x'][l]), lru_lam=r1(P['lru_lambda'][l]),
        gains=[r1(P['grp_norm_g'][l][i]) for i in range(3)],
        dn_cw=P['dn_conv_w'][l], dn_alog=small_row(P['dn_a_log'][l], N_HEADS),
        dn_dtb=small_row(P['dn_dt_bias'][l], N_HEADS), dn_ng=r1(jnp.tile(P['dn_norm_g'][l], N_HEADS)),
        w_o=P['w_o'][l].astype(bf16), ln1_g=r1(P['ln1_g'][l]), ln1_b=r1(P['ln1_b'][l]),
        ffn_win=P['ffn_w_in'][l].astype(bf16), ffn_cw=P['ffn_conv_w'][l], ffn_cb=r1(P['ffn_conv_b'][l]),
        ffn_wout=P['ffn_w_out'][l].astype(bf16), ln2_g=r1(P['ln2_g'][l]), ln2_b=r1(P['ln2_b'][l]),
    )


def _embed_block_diag(S):
    B = S.shape[0]
    eye = jnp.eye(N_HEADS, dtype=S.dtype)
    return (S[:, :, :, None, :] * eye[None, :, None, :, None]).reshape(B, GROUP_W, GROUP_W)


def _extract_block_diag(Sbd):
    return jnp.stack([Sbd[:, h * HEAD_DIM:(h + 1) * HEAD_DIM, h * HEAD_DIM:(h + 1) * HEAD_DIM]
                      for h in range(N_HEADS)], axis=1)


def _trunk(x, st, LW, ln_in, alpha):
    B, L, D = x.shape
    W = GROUP_W
    F = LW[0]['ffn_wout'].shape[0]
    new = {n: [] for n in ('lru_conv', 'lru_h', 'fox_k', 'fox_v', 'fox_logf', 'sb_k', 'sb_v',
                           'dn_conv', 'dn_S', 'ffn_conv')}
    x2 = x.reshape(B * L, D)
    for l, w in enumerate(LW):
        outs = _proj(x2, ln_in[0], ln_in[1], w['w_in'], w['b_in'], do_ln=(l == 0))
        if l == 0:
            x2, outs = outs[0], outs[1:]
        lru, fq, fk, fv, sq, sk, sv, dqkv, dgate, small = [o.reshape(B, L, -1) for o in outs]
        if st is None:
            lru_conv = jnp.zeros((B, 3, W), f32)
            lru_h = jnp.zeros((B, 1, W), f32)
            dn_conv = jnp.zeros((B, 3, 3 * W), f32)
            dn_S = jnp.zeros((B, W, W), f32)
            ffn_conv = jnp.zeros((B, 2, F), f32)
            pfk = pfv = plf = psk = psv = None
        else:
            Pn = st['fox_k'].shape[2]
            lru_conv = st['lru_conv'][l]
            lru_h = st['lru_h'][l].reshape(B, 1, W)
            dn_conv = st['dn_conv'][l]
            dn_S = _embed_block_diag(st['dn_S'][l])
            ffn_conv = st['ffn_conv'][l]
            feat_major = lambda c: jnp.transpose(c.reshape(len(LW), B, Pn, W), (0, 1, 3, 2))
            pfk = feat_major(st['fox_k'])
            pfv = feat_major(st['fox_v'])
            plf = jnp.pad(st['fox_logf'][l], ((0, 0), (0, 0), (0, SMALL_W - N_HEADS)))
            psk = feat_major(st['sb_k'])
            psv = feat_major(st['sb_v'])

        ya, lru_conv_new, h_last = _lru(lru, lru_conv, lru_h, w['lru_cw'], w['lru_cb'], w['lru_wa'],
                                        w['lru_wx'], w['lru_ba'], w['lru_bx'], w['lru_lam'], w['gains'][0])
        kv_t = L >= ATT_SUB
        yb, logf, *fox_kv = _fox(fq, fk, fv, small, pfk, pfv, plf, w['gains'][1], kv_t, l)
        yc, *sb_kv = _sb(sq, sk, sv, psk, psv, w['gains'][2], kv_t, l)
        yd, dn_conv_new, S_new = _dn(dqkv, dgate, small, dn_conv, dn_S, w['dn_cw'], w['dn_alog'],
                                     w['dn_dtb'], w['dn_ng'])
        x1 = _merge(alpha, x2, ya.reshape(B * L, W), yb.reshape(B * L, W), yc.reshape(B * L, W),
                    yd.reshape(B * L, W), w['w_o'], w['ln1_g'], w['ln1_b'])
        x3, ffn_new = _ffn(alpha, x1.reshape(B, L, D), ffn_conv, w['ffn_win'], w['ffn_cw'], w['ffn_cb'],
                           w['ffn_wout'], w['ln2_g'], w['ln2_b'])
        x2 = x3.reshape(B * L, D)

        if kv_t:
            heads = lambda a: jnp.transpose(a.reshape(B, N_HEADS, HEAD_DIM, L), (0, 3, 1, 2))
            kv = fox_kv + sb_kv
        else:
            heads = lambda a: a.reshape(B, L, N_HEADS, HEAD_DIM)
            kv = [fk, fv, sk, sv]
        new['lru_conv'].append(lru_conv_new)
        new['lru_h'].append(h_last.reshape(B, W))
        new['fox_k'].append(heads(kv[0]))
        new['fox_v'].append(heads(kv[1]))
        new['fox_logf'].append(logf[:, :, :N_HEADS])
        new['sb_k'].append(heads(kv[2]))
        new['sb_v'].append(heads(kv[3]))
        new['dn_conv'].append(dn_conv_new)
        new['dn_S'].append(_extract_block_diag(S_new))
        new['ffn_conv'].append(ffn_new)
    return x2.reshape(B, L, D), {n: jnp.stack(v) for n, v in new.items()}


def kernel(x_prompt, x_sample, state_lru_conv, state_lru_h, cache_fox_k, cache_fox_v, cache_fox_logf,
           cache_sb_k, cache_sb_v, state_dn_conv, state_dn_S, state_ffn_conv, meta_tokens, ln_in_g, ln_in_b,
           w_in, b_in, lru_conv_w, lru_conv_b, lru_w_a, lru_b_a, lru_w_x, lru_b_x, lru_lambda, dn_conv_w,
           dn_a_log, dn_dt_bias, dn_norm_g, grp_norm_g, w_o, ln1_g, ln1_b, ffn_w_in, ffn_conv_w, ffn_conv_b,
           ffn_w_out, ln2_g, ln2_b):
    P = dict(w_in=w_in, b_in=b_in, lru_conv_w=lru_conv_w, lru_conv_b=lru_conv_b, lru_w_a=lru_w_a,
             lru_b_a=lru_b_a, lru_w_x=lru_w_x, lru_b_x=lru_b_x, lru_lambda=lru_lambda, dn_conv_w=dn_conv_w,
             dn_a_log=dn_a_log, dn_dt_bias=dn_dt_bias, dn_norm_g=dn_norm_g, grp_norm_g=grp_norm_g, w_o=w_o,
             ln1_g=ln1_g, ln1_b=ln1_b, ffn_w_in=ffn_w_in, ffn_conv_w=ffn_conv_w, ffn_conv_b=ffn_conv_b,
             ffn_w_out=ffn_w_out, ln2_g=ln2_g, ln2_b=ln2_b)
    depth = w_in.shape[0]
    assert x_prompt.shape[2] == D_MODEL
    alpha = (2.0 * depth) ** 0.25
    LW = [_prep_layer_weights(l, P) for l in range(depth)]
    ln_in = (ln_in_g.reshape(1, -1), ln_in_b.reshape(1, -1))

    bp = x_prompt.shape[0]
    meta = jnp.broadcast_to(meta_tokens.astype(x_prompt.dtype), (bp, N_META, D_MODEL))
    xp = jnp.concatenate([meta, x_prompt], axis=1)
    yp, ps = _trunk(xp, None, LW, ln_in, alpha)

    st_in = dict(lru_conv=state_lru_conv, lru_h=state_lru_h, fox_k=cache_fox_k, fox_v=cache_fox_v,
                 fox_logf=cache_fox_logf, sb_k=cache_sb_k, sb_v=cache_sb_v, dn_conv=state_dn_conv,
                 dn_S=state_dn_S, ffn_conv=state_ffn_conv)
    ys, ss = _trunk(x_sample, st_in, LW, ln_in, alpha)

    names = ('lru_conv', 'lru_h', 'fox_k', 'fox_v', 'fox_logf', 'sb_k', 'sb_v', 'dn_conv', 'dn_S', 'ffn_conv')
    return (yp[:, N_META:], ys) + tuple(ps[n] for n in names) + tuple(ss[n] for n in names)
```

```python
import functools

import jax
import jax.numpy as jnp
from jax import lax
from jax.experimental import pallas as pl
from jax.experimental.pallas import tpu as pltpu

f32 = jnp.float32
bf16 = jnp.bfloat16

D_MODEL = 1024
GROUP_W = 256
N_HEADS = 4
HEAD_DIM = 64
N_META = 16
LRU_C = 8.0
LN_EPS = 1e-5
RMS_EPS = 1e-6
ATT_TILE = 256
ATT_SUB = 128
DN_CHUNK = 64
DN_GROUP = 8
NEG = -1e30
LOG2E = 1.4426950408889634
SMALL_W = 128
VMEM_LIMIT_BYTES = 56 * 1024 * 1024

NN = (((1,), (0,)), ((), ()))
NT = (((1,), (1,)), ((), ()))
TN = (((0,), (0,)), ((), ()))


def _dot(a, b, dims=NN):
    return lax.dot_general(a, b, dims, preferred_element_type=f32)


def _split2(x):
    hi = x.astype(bf16)
    lo = (x - hi.astype(f32)).astype(bf16)
    return hi, lo


def _split3(x):
    p0 = x.astype(bf16)
    r = x - p0.astype(f32)
    p1 = r.astype(bf16)
    p2 = (r - p1.astype(f32)).astype(bf16)
    return p0, p1, p2


def _dot_sel_rhs(x, sel, dims=NN):
    p0, p1, p2 = _split3(x)
    return _dot(p0, sel, dims) + _dot(p1, sel, dims) + _dot(p2, sel, dims)


def _dot_sel_lhs(sel, x, dims=NN):
    p0, p1, p2 = _split3(x)
    return _dot(sel, p0, dims) + _dot(sel, p1, dims) + _dot(sel, p2, dims)


def _iota(shape, dim):
    return lax.broadcasted_iota(jnp.int32, shape, dim)


def _head_masks(width=GROUP_W):
    lane_head = _iota((1, width), 1) >> 6
    return [lane_head == h for h in range(N_HEADS)]


def _softplus(x):
    return jnp.maximum(x, 0.0) + jnp.log1p(jnp.exp(-jnp.abs(x)))


def _log_sigmoid(x):
    return jnp.minimum(x, 0.0) - jnp.log1p(jnp.exp(-jnp.abs(x)))


def _layer_norm(x, g, b):
    mu = jnp.mean(x, -1, keepdims=True)
    xc = x - mu
    var = jnp.mean(xc * xc, -1, keepdims=True)
    return xc * lax.rsqrt(var + LN_EPS) * g + b


def _rms_norm(x, g):
    return x * lax.rsqrt(jnp.mean(x * x, -1, keepdims=True) + RMS_EPS) * g


def _row_tile(n, cap, mult=16):
    best = None
    for d in range(mult, min(n, cap) + 1, mult):
        if n % d == 0:
            best = d
    return best if best is not None else n


def _const_spec(shape):
    nd = len(shape)
    return pl.BlockSpec(shape, lambda *_: (0,) * nd, pipeline_mode=pl.Buffered(1))


def _layer_spec(shape, layer):
    nd = len(shape)
    return pl.BlockSpec((None,) + shape, lambda *_: (layer,) + (0,) * nd, pipeline_mode=pl.Buffered(1))


def _params(*sem):
    return pltpu.CompilerParams(dimension_semantics=sem, vmem_limit_bytes=VMEM_LIMIT_BYTES)


PROJ_WIDTHS = (512, 256, 256, 256, 256, 256, 256, 768, 256, SMALL_W)


def _proj_kernel(do_ln, x_ref, g_ref, b_ref, w_ref, bias_ref, *outs):
    x = x_ref[...]
    if do_ln:
        x = _layer_norm(x, g_ref[...], b_ref[...])
        outs[0][...] = x
        outs = outs[1:]
    xb = x.astype(bf16)
    col = 0
    for o, wd in zip(outs, PROJ_WIDTHS):
        o[...] = _dot(xb, w_ref[col:col + wd, :], NT) + bias_ref[:, col:col + wd]
        col += wd


def _proj(x2d, ln_g, ln_b, w, bias, do_ln):
    n = x2d.shape[0]
    tm = _row_tile(n, 768)
    wtot = w.shape[0]
    row = lambda wd: pl.BlockSpec((tm, wd), lambda i: (i, 0))
    out_shape = [jax.ShapeDtypeStruct((n, wd), f32) for wd in PROJ_WIDTHS]
    out_specs = [row(wd) for wd in PROJ_WIDTHS]
    if do_ln:
        out_shape = [jax.ShapeDtypeStruct((n, D_MODEL), f32)] + out_shape
        out_specs = [row(D_MODEL)] + out_specs
    return pl.pallas_call(
        functools.partial(_proj_kernel, do_ln),
        grid=(n // tm,),
        in_specs=[row(D_MODEL), _const_spec((1, D_MODEL)), _const_spec((1, D_MODEL)),
                  _const_spec((wtot, D_MODEL)), _const_spec((1, wtot))],
        out_specs=out_specs,
        out_shape=out_shape,
        compiler_params=_params("parallel"),
        name="proj",
    )(x2d, ln_g, ln_b, w, bias)


def _lru_kernel(L, R1, lru_ref, cst_ref, h0_ref, cw_ref, cb_ref, wa_ref, wx_ref, ba_ref, bx_ref,
                lam_ref, gain_ref, ya_ref, cnew_ref, hlast_ref, xpad_s, a_s, b_s):
    W = GROUP_W
    xpad_s[pl.ds(0, 8), :] = jnp.zeros((8, W), f32)
    xpad_s[pl.ds(5, 3), :] = cst_ref[0]
    xpad_s[pl.ds(8, L), :] = lru_ref[0, :, 0:W]
    cnew_ref[0] = xpad_s[pl.ds(L + 5, 3), :]

    sp = _softplus(-lam_ref[...])
    cw = cw_ref[...]
    for t in range(L // R1):
        r0 = t * R1
        xa = cb_ref[...]
        for i in range(4):
            xa = xa + cw[i:i + 1, :] * xpad_s[pl.ds(5 + r0 + i, R1), :]
        xb = xa.astype(bf16)
        r = jax.nn.sigmoid(_dot(xb, wa_ref[...]) + ba_ref[...])
        ig = jax.nn.sigmoid(_dot(xb, wx_ref[...]) + bx_ref[...])
        log_a = (-LRU_C) * r * sp
        y2 = 2.0 * log_a
        one_m_a2 = jnp.tanh(-0.5 * y2) * (1.0 + jnp.exp(y2))
        a_s[pl.ds(r0, R1), :] = jnp.exp(log_a)
        b_s[pl.ds(r0, R1), :] = jnp.sqrt(one_m_a2) * (ig * xa)

    rows8 = _iota((8, W), 0)

    def scan_body(g, hprev):
        off = pl.multiple_of(g * 8, 8)
        A = a_s[pl.ds(off, 8), :]
        Bv = b_s[pl.ds(off, 8), :]
        for s in (1, 2, 4):
            keep = rows8 >= s
            a_sh = jnp.where(keep, pltpu.roll(A, s, 0), 1.0)
            b_sh = jnp.where(keep, pltpu.roll(Bv, s, 0), 0.0)
            Bv = A * b_sh + Bv
            A = A * a_sh
        h = A * hprev + Bv
        b_s[pl.ds(off, 8), :] = h
        return h[7:8, :]

    hlast_ref[0] = lax.fori_loop(0, L // 8, scan_body, h0_ref[0])

    for t in range(L // R1):
        r0 = t * R1
        o = jax.nn.gelu(lru_ref[0, pl.ds(r0, R1), W:2 * W]) * b_s[pl.ds(r0, R1), :]
        ya_ref[0, pl.ds(r0, R1), :] = _rms_norm(o, gain_ref[...]).astype(bf16)


def _lru(lru3d, conv_st, h0, cw, cb, wa, wx, ba, bx, lam, gain):
    B, L, _ = lru3d.shape
    R1 = _row_tile(L, 768)
    W = GROUP_W
    per_b = lambda shape: pl.BlockSpec((1,) + shape, lambda b: (b,) + (0,) * len(shape))
    return pl.pallas_call(
        functools.partial(_lru_kernel, L, R1),
        grid=(B,),
        in_specs=[per_b((L, 2 * W)), per_b((3, W)), per_b((1, W)),
                  _const_spec((4, W)), _const_spec((1, W)), _const_spec((W, W)), _const_spec((W, W)),
                  _const_spec((1, W)), _const_spec((1, W)), _const_spec((1, W)), _const_spec((1, W))],
        out_specs=[per_b((L, W)), per_b((3, W)), per_b((1, W))],
        out_shape=[jax.ShapeDtypeStruct((B, L, W), bf16), jax.ShapeDtypeStruct((B, 3, W), f32),
                   jax.ShapeDtypeStruct((B, 1, W), f32)],
        scratch_shapes=[pltpu.VMEM((L + 8, W), f32), pltpu.VMEM((L, W), f32), pltpu.VMEM((L, W), f32)],
        compiler_params=_params("parallel"),
        name="lru",
    )(lru3d, conv_st, h0, cw, cb, wa, wx, ba, bx, lam, gain)


def _att_layout(L, P):
    T = ATT_TILE
    assert (P + L) % 16 == 0
    front = (-(P + L)) % T
    Lpad = front + P + L
    L0 = L % T
    nT = L // T
    base_t = (front + P + L0) // T
    return front, Lpad, L0, nT, base_t


def _stage_pairs(front, P, L, new_ref, past_ref, flat_s, pair_s, hm):
    W = GROUP_W
    S = ATT_SUB
    if front:
        flat_s[pl.ds(0, front), :] = jnp.zeros((front, W), bf16)
    if P:
        assert P % S == 0

        def past_body(u, c):
            r = pl.multiple_of(u * S, S)
            flat_s[pl.ds(pl.multiple_of(front + r, 16), S), :] = past_ref[0, :, pl.ds(r, S)].T.astype(bf16)
            return c

        lax.fori_loop(0, P // S, past_body, 0)
    flat_s[pl.ds(front + P, L), :] = new_ref[0].astype(bf16)
    zero = jnp.zeros((S, W), bf16)

    def body(u, c):
        x = flat_s[pl.ds(pl.multiple_of(u * S, S), S), :]
        base = pl.multiple_of(u * 2 * S, 2 * S)
        for p in range(2):
            pair_s[p, pl.ds(base, S), :] = jnp.where(hm[2 * p], x, zero)
            pair_s[p, pl.ds(base + S, S), :] = jnp.where(hm[2 * p + 1], x, zero)
        return c

    lax.fori_loop(0, (front + P + L) // S, body, 0)


def _store_transposed(src_ref, dst_ref, L):
    S = ATT_SUB
    n_full = L // S

    def body(u, c):
        r = pl.multiple_of(u * S, S)
        dst_ref[0, :, pl.ds(r, S)] = src_ref[0, pl.ds(r, S), :].T
        return c

    lax.fori_loop(0, n_full, body, 0)
    tail = L - n_full * S
    if tail:
        eye = (_iota((GROUP_W, GROUP_W), 0) == _iota((GROUP_W, GROUP_W), 1)).astype(bf16)
        dst_ref[0, :, pl.ds(n_full * S, tail)] = _dot_sel_lhs(eye, src_ref[0, pl.ds(n_full * S, tail), :], NT)


def _pair_scores(qb, kpair_s, j):
    S2 = 2 * ATT_SUB
    kbase = j * (2 * ATT_TILE)
    return [[_dot(qb, kpair_s[p, pl.ds(pl.multiple_of(kbase + sub * S2, S2), S2), :], NT)
             for sub in range(2)] for p in range(2)]


def _pair_pv(wb, vpair_s, j):
    S2 = 2 * ATT_SUB
    kbase = j * (2 * ATT_TILE)
    out = None
    for sub in range(2):
        for p in range(2):
            lhs = jnp.concatenate([wb[2 * p][sub], wb[2 * p + 1][sub]], axis=1)
            t = _dot(lhs, vpair_s[p, pl.ds(pl.multiple_of(kbase + sub * S2, S2), S2), :])
            out = t if out is None else out + t
    return out


def _by_head_lanes(vals, lo_half):
    return jnp.concatenate([jnp.where(lo_half, vals[0], vals[1]), jnp.where(lo_half, vals[2], vals[3])], axis=1)


def _fox_kernel(L, P, emit_t, *refs):
    T, S = ATT_TILE, ATT_SUB
    front, Lpad, L0, nT, base_t = _att_layout(L, P)
    n_in = 8 if P else 5
    n_out = 4 if emit_t else 2
    if P:
        q_ref, k_ref, v_ref, sm_ref, pk_ref, pv_ref, plf_ref, gain_ref = refs[:n_in]
    else:
        q_ref, k_ref, v_ref, sm_ref, gain_ref = refs[:n_in]
        pk_ref = pv_ref = plf_ref = None
    yb_ref, lf_ref = refs[n_in:n_in + 2]
    scr = refs[n_in + n_out:]
    flat_s, kpair_s, vpair_s, lfp_s, F_s, FT_s, qb_s, fqb_s, m_s, lp_s, acc_s, cb_s = scr
    hm = _head_masks()
    lo_half = _iota((1, S), 1) < HEAD_DIM
    scale = HEAD_DIM ** -0.5 * LOG2E
    if emit_t:
        _store_transposed(k_ref, refs[n_in + 2], L)
        _store_transposed(v_ref, refs[n_in + 3], L)

    _stage_pairs(front, P, L, k_ref, pk_ref, flat_s, kpair_s, hm)
    _stage_pairs(front, P, L, v_ref, pv_ref, flat_s, vpair_s, hm)

    if front:
        lfp_s[pl.ds(0, front), :] = jnp.zeros((front, SMALL_W), f32)
    if P:
        lfp_s[pl.ds(front, P), :] = plf_ref[0]
    lf = _log_sigmoid(sm_ref[0])
    lf_ref[0] = lf
    lfp_s[pl.ds(front + P, L), :] = lf
    ltri = (_iota((S, S), 0) >= _iota((S, S), 1)).astype(bf16)
    sel8 = (_iota((8, SMALL_W), 0) == _iota((8, SMALL_W), 1)).astype(bf16)

    n_ct = Lpad // S
    local = [_dot_sel_lhs(ltri, lfp_s[pl.ds(u * S, S), :]) for u in range(n_ct)]
    offset = jnp.zeros((1, SMALL_W), f32)
    F2 = []
    for u in range(n_ct):
        F2.append((local[u] + offset) * LOG2E)
        offset = offset + local[u][S - 1:S, :]
    for u in range(n_ct):
        F_s[pl.ds(u * S, S), :] = F2[u]
    for u in range(n_ct):
        FT_s[:, pl.ds(u * S, S)] = _dot_sel_lhs(sel8, F2[u], NT)
    if front:
        FT_s[:, pl.ds(0, front)] = jnp.full((8, front), -NEG, f32)
    cb_s[...] = jnp.where(_iota((T, T), 1) <= _iota((T, T), 0), 0.0, NEG)

    def process(qnat, tq, d):
        rows = pl.ds(0, tq)
        qp0 = front + P + qnat
        qb_s[rows, :] = (q_ref[0, pl.ds(qnat, tq), :] * scale).astype(bf16)
        Fq = F_s[pl.ds(qp0, tq), :]
        for h in range(N_HEADS):
            fqb_s[h, rows, :] = jnp.broadcast_to(Fq[:, h:h + 1], (tq, S))
            m_s[h, rows, :] = jnp.full((tq, S), NEG, f32)
            lp_s[h, rows, :] = jnp.zeros((tq, S), f32)
        acc_s[rows, :] = jnp.zeros((tq, GROUP_W), f32)

        def scores(j):
            return _pair_scores(qb_s[rows, :], kpair_s, j)

        def update(j, sc, mode):
            if mode == 'ragged':
                qpos = qp0 + _iota((tq, S), 0)
                vis = []
                for sub in range(2):
                    kpos = j * T + sub * S + _iota((tq, S), 1)
                    ok = kpos <= qpos
                    if front:
                        ok = ok & (kpos >= front)
                    vis.append(ok)
            alphas, pb = [], []
            for h in range(N_HEADS):
                p_, half = divmod(h, 2)
                fq = fqb_s[h, rows, :]
                c = []
                for sub in range(2):
                    fk = FT_s[h:h + 1, pl.ds(pl.multiple_of(j * T + sub * S, S), S)]
                    x = sc[p_][sub][:, half * S:(half + 1) * S] + (fq - fk)
                    if mode == 'ragged':
                        x = jnp.where(vis[sub], x, NEG)
                    elif mode == 'diag':
                        x = x + cb_s[:, sub * S:(sub + 1) * S]
                    c.append(x)
                m_old = m_s[h, rows, :]
                m_new = jnp.maximum(m_old, jnp.max(jnp.maximum(c[0], c[1]), -1, keepdims=True))
                alpha = jnp.exp2(m_old - m_new)
                e0 = jnp.exp2(c[0] - m_new)
                e1 = jnp.exp2(c[1] - m_new)
                lp_s[h, rows, :] = alpha * lp_s[h, rows, :] + (e0 + e1)
                m_s[h, rows, :] = m_new
                alphas.append(alpha)
                pb.append((e0.astype(bf16), e1.astype(bf16)))
            pv = _pair_pv(pb, vpair_s, j)
            acc_s[rows, :] = acc_s[rows, :] * _by_head_lanes(alphas, lo_half) + pv

        def pair(j0, mode0, j1, mode1):
            sc0 = scores(j0)
            sc1 = scores(j1)
            update(j0, sc0, mode0)
            update(j1, sc1, mode1)

        def plain2(t, c):
            pair(2 * t, 'plain', 2 * t + 1, 'plain')
            return c

        lax.fori_loop(0, d // 2, plain2, 0)
        if isinstance(d, int):
            if d % 2:
                pair(d - 1, 'plain', d, 'ragged')
            else:
                update(d, scores(d), 'ragged')
        else:
            @pl.when(d % 2 == 1)
            def _():
                pair(d - 1, 'plain', d, 'diag')

            @pl.when(d % 2 == 0)
            def _():
                update(d, scores(d), 'diag')

        l = [jnp.sum(lp_s[h, rows, :], -1, keepdims=True) for h in range(N_HEADS)]
        out = acc_s[rows, :] / _by_head_lanes(l, lo_half)
        yb_ref[0, pl.ds(qnat, tq), :] = _rms_norm(out, gain_ref[...]).astype(bf16)

    if L0:
        process(0, L0, base_t - 1)

    def q_body(i, c):
        process(pl.multiple_of(L0 + i * T, 16), T, base_t + i)
        return c

    lax.fori_loop(0, nT, q_body, 0)


def _fox(q, k, v, small, past_k, past_v, past_lf, gain, emit_t, layer):
    B, L, W = q.shape
    P = 0 if past_k is None else past_k.shape[3]
    T = ATT_TILE
    _, Lpad, _, _, _ = _att_layout(L, P)
    per_b = lambda shape: pl.BlockSpec((1,) + shape, lambda b: (b,) + (0,) * len(shape))
    per_lb = lambda shape: pl.BlockSpec((None, 1) + shape, lambda b: (layer, b) + (0,) * len(shape))
    in_specs = [per_b((L, W)), per_b((L, W)), per_b((L, W)), per_b((L, SMALL_W))]
    args = [q, k, v, small]
    if P:
        in_specs += [per_lb((W, P)), per_lb((W, P)), per_b((P, SMALL_W))]
        args += [past_k, past_v, past_lf]
    in_specs.append(_const_spec((1, W)))
    args.append(gain)
    out_specs = [per_b((L, W)), per_b((L, SMALL_W))]
    out_shape = [jax.ShapeDtypeStruct((B, L, W), bf16), jax.ShapeDtypeStruct((B, L, SMALL_W), f32)]
    if emit_t:
        out_specs += [per_b((W, L)), per_b((W, L))]
        out_shape += [jax.ShapeDtypeStruct((B, W, L), f32)] * 2
    return pl.pallas_call(
        functools.partial(_fox_kernel, L, P, emit_t),
        grid=(B,),
        in_specs=in_specs,
        out_specs=out_specs,
        out_shape=out_shape,
        scratch_shapes=[pltpu.VMEM((Lpad, W), bf16), pltpu.VMEM((2, 2 * Lpad, W), bf16),
                        pltpu.VMEM((2, 2 * Lpad, W), bf16), pltpu.VMEM((Lpad, SMALL_W), f32),
                        pltpu.VMEM((Lpad, SMALL_W), f32), pltpu.VMEM((8, Lpad), f32),
                        pltpu.VMEM((T, W), bf16), pltpu.VMEM((N_HEADS, T, ATT_SUB), f32),
                        pltpu.VMEM((N_HEADS, T, ATT_SUB), f32), pltpu.VMEM((N_HEADS, T, ATT_SUB), f32),
                        pltpu.VMEM((T, W), f32), pltpu.VMEM((T, T), f32)],
        compiler_params=_params("parallel"),
        name="fox",
    )(*args)


def _sb_kernel(L, P, emit_t, *refs):
    T, S = ATT_TILE, ATT_SUB
    front, Lpad, L0, nT, base_t = _att_layout(L, P)
    n_in = 6 if P else 4
    n_out = 3 if emit_t else 1
    if P:
        q_ref, k_ref, v_ref, pk_ref, pv_ref, gain_ref = refs[:n_in]
    else:
        q_ref, k_ref, v_ref, gain_ref = refs[:n_in]
        pk_ref = pv_ref = None
    yc_ref = refs[n_in]
    scr = refs[n_in + n_out:]
    flat_s, kpair_s, vpair_s, qb_s, c_s, acc_s, cb_s = scr
    hm = _head_masks()
    scale = HEAD_DIM ** -0.5 * LOG2E
    if emit_t:
        _store_transposed(k_ref, refs[n_in + 1], L)
        _store_transposed(v_ref, refs[n_in + 2], L)

    _stage_pairs(front, P, L, k_ref, pk_ref, flat_s, kpair_s, hm)
    _stage_pairs(front, P, L, v_ref, pv_ref, flat_s, vpair_s, hm)

    after2 = ((_iota((2 * T, T), 0) & (T - 1)) >= _iota((2 * T, T), 1)).astype(bf16)
    cb_s[...] = jnp.where(_iota((T, T), 1) < _iota((T, T), 0), 0.0, NEG)
    pad_bias = jnp.where(_iota((1, T), 1) >= front, 0.0, NEG)

    def process(qnat, tq, d):
        rows = pl.ds(0, tq)
        qp0 = front + P + qnat
        qb_s[rows, :] = (q_ref[0, pl.ds(qnat, tq), :] * scale).astype(bf16)
        for h in range(N_HEADS):
            c_s[h, rows, :] = jnp.zeros((tq, S), f32)
        acc_s[rows, :] = jnp.zeros((tq, GROUP_W), f32)

        def scores(j):
            return _pair_scores(qb_s[rows, :], kpair_s, j)

        def update(j, sc, mode):
            if mode == 'ragged':
                kpos = j * T + _iota((tq, T), 1)
                vis = kpos < qp0 + _iota((tq, T), 0)
                if front:
                    vis = vis & (kpos >= front)
            wb = []
            for h in range(N_HEADS):
                p_, half = divmod(h, 2)
                z = jnp.concatenate([sc[p_][sub][:, half * S:(half + 1) * S] for sub in range(2)], axis=1)
                if mode == 'diag':
                    z = z + cb_s[...]
                elif mode == 'pad':
                    z = z + pad_bias
                nk = jnp.maximum(z, 0.0) + jnp.log2(1.0 + jnp.exp2(-jnp.abs(z)))
                if mode == 'ragged':
                    nk = jnp.where(vis, nk, 0.0)
                hi, lo = _split2(nk)
                incl = _dot(jnp.concatenate([hi, lo], axis=1), after2)
                c = c_s[h, rows, :]
                w = jnp.exp2(z - incl - jnp.concatenate([c, c], axis=1))
                if mode == 'ragged':
                    w = jnp.where(vis, w, 0.0)
                c_s[h, rows, :] = c + jnp.sum(nk, -1, keepdims=True)
                wbf = w.astype(bf16)
                wb.append((wbf[:, 0:S], wbf[:, S:2 * S]))
            acc_s[rows, :] = acc_s[rows, :] + _pair_pv(wb, vpair_s, j)

        def kv_step(j, mode):
            update(j, scores(j), mode)

        def plain2(t, c):
            j = d - 1 - 2 * t
            sc0 = scores(j)
            sc1 = scores(j - 1)
            update(j, sc0, 'plain')
            update(j - 1, sc1, 'plain')
            return c

        jf = 1 if front else 0
        kv_step(d, 'ragged' if isinstance(d, int) else 'diag')
        if (not isinstance(d, int)) or d > 0:
            n_plain = d - jf
            lax.fori_loop(0, n_plain // 2, plain2, 0)

            @pl.when(n_plain % 2 == 1)
            def _():
                kv_step(jf, 'plain')
            if front:
                kv_step(0, 'pad')

        yc_ref[0, pl.ds(qnat, tq), :] = _rms_norm(acc_s[rows, :], gain_ref[...]).astype(bf16)

    if L0:
        process(0, L0, base_t - 1)

    def q_body(i, c):
        process(pl.multiple_of(L0 + i * T, 16), T, base_t + i)
        return c

    lax.fori_loop(0, nT, q_body, 0)


def _sb(q, k, v, past_k, past_v, gain, emit_t, layer):
    B, L, W = q.shape
    P = 0 if past_k is None else past_k.shape[3]
    per_lb = lambda shape: pl.BlockSpec((None, 1) + shape, lambda b: (layer, b) + (0,) * len(shape))
    T = ATT_TILE
    _, Lpad, _, _, _ = _att_layout(L, P)
    per_b = lambda shape: pl.BlockSpec((1,) + shape, lambda b: (b,) + (0,) * len(shape))
    in_specs = [per_b((L, W)), per_b((L, W)), per_b((L, W))]
    args = [q, k, v]
    if P:
        in_specs += [per_lb((W, P)), per_lb((W, P))]
        args += [past_k, past_v]
    in_specs.append(_const_spec((1, W)))
    args.append(gain)
    out_specs = [per_b((L, W))]
    out_shape = [jax.ShapeDtypeStruct((B, L, W), bf16)]
    if emit_t:
        out_specs += [per_b((W, L)), per_b((W, L))]
        out_shape += [jax.ShapeDtypeStruct((B, W, L), f32)] * 2
    return pl.pallas_call(
        functools.partial(_sb_kernel, L, P, emit_t),
        grid=(B,),
        in_specs=in_specs,
        out_specs=out_specs,
        out_shape=out_shape,
        scratch_shapes=[pltpu.VMEM((Lpad, W), bf16), pltpu.VMEM((2, 2 * Lpad, W), bf16),
                        pltpu.VMEM((2, 2 * Lpad, W), bf16), pltpu.VMEM((T, W), bf16),
                        pltpu.VMEM((N_HEADS, T, ATT_SUB), f32), pltpu.VMEM((T, W), f32),
                        pltpu.VMEM((T, T), f32)],
        compiler_params=_params("parallel"),
        name="sb",
    )(*args)


def _dn_kernel(L, R1, qkv_ref, gate_ref, sm_ref, cst_ref, s0_ref, cw_ref, alog_ref, dtb_ref, ng_ref,
               yd_ref, cnew_ref, sout_ref, xpad_s, q_s, k_s, v_s, g_s, beta_s, o_s, S_s):
    W = GROUP_W
    hm = _head_masks()
    bd_ones = ((_iota((W, W), 0) >> 6) == (_iota((W, W), 1) >> 6)).astype(bf16)
    exp_g = (_iota((SMALL_W, W), 0) == (_iota((SMALL_W, W), 1) >> 6) + N_HEADS).astype(bf16)
    exp_b = (_iota((SMALL_W, W), 0) == (_iota((SMALL_W, W), 1) >> 6) + 2 * N_HEADS).astype(bf16)

    def head_sum(x):
        hi, lo = _split2(x)
        return _dot(hi, bd_ones) + _dot(lo, bd_ones)

    xpad_s[pl.ds(0, 8), :] = jnp.zeros((8, 3 * W), f32)
    xpad_s[pl.ds(5, 3), :] = cst_ref[0]
    xpad_s[pl.ds(8, L), :] = qkv_ref[0]
    cnew_ref[0] = xpad_s[pl.ds(L + 5, 3), :]
    cw = cw_ref[...]
    neg_a = -jnp.exp(alog_ref[...])
    for t in range(L // R1):
        r0 = t * R1
        rows = pl.ds(r0, R1)
        xc = cw[0:1, :] * xpad_s[pl.ds(5 + r0, R1), :]
        for i in range(1, 4):
            xc = xc + cw[i:i + 1, :] * xpad_s[pl.ds(5 + r0 + i, R1), :]
        xc = xc * jax.nn.sigmoid(xc)
        q = xc[:, 0:W]
        k = xc[:, W:2 * W]
        q_s[rows, :] = q * lax.rsqrt(head_sum(q * q) + RMS_EPS) * (HEAD_DIM ** -0.5)
        k_s[rows, :] = k * lax.rsqrt(head_sum(k * k) + RMS_EPS)
        v_s[rows, :] = xc[:, 2 * W:3 * W]
        sm = sm_ref[0, rows, :]
        g_s[rows, :] = _dot_sel_rhs(neg_a * _softplus(sm + dtb_ref[...]), exp_g)
        beta_s[rows, :] = _dot_sel_rhs(jax.nn.sigmoid(sm), exp_b)

    S_s[...] = s0_ref[0]
    bd_mask = (_iota((W, W), 0) >> 6) == (_iota((W, W), 1) >> 6)

    def part_a(offs, C):
        each = lambda f, *ls: [f(*a) for a in zip(*ls)]
        CW = N_HEADS * C
        assert C & (C - 1) == 0
        ri = _iota((C, CW), 0)
        si = _iota((C, CW), 1) & (C - 1)
        blk = [(_iota((1, CW), 1) >> (C.bit_length() - 1)) == h for h in range(N_HEADS)]
        ltri = (_iota((C, C), 0) >= _iota((C, C), 1)).astype(bf16)

        def head_rows(x):
            xb = x.astype(bf16)
            return jnp.concatenate([jnp.where(hm[h], xb, jnp.zeros_like(xb)) for h in range(N_HEADS)], axis=0)

        def head_blocks(y):
            yb = y.astype(bf16)
            return jnp.concatenate([jnp.where(blk[h], yb, jnp.zeros_like(yb)) for h in range(N_HEADS)], axis=0)

        q = [q_s[pl.ds(off, C), :] for off in offs]
        k = [k_s[pl.ds(off, C), :] for off in offs]
        beta = [beta_s[pl.ds(off, C), :] for off in offs]
        gc = [_dot_sel_lhs(ltri, g_s[pl.ds(off, C), :]) for off in offs]
        eg = each(jnp.exp, gc)
        g_last = [g[C - 1:C, :] for g in gc]
        kb = each(lambda a, b: a * b, k, beta)
        vb = [v_s[pl.ds(off, C), :] * b for off, b in zip(offs, beta)]
        kbe = each(lambda a, b: a * b, kb, eg)
        yield

        gparts = each(_split3, gc)
        first_lane = ((_iota((C, W), 1) & (HEAD_DIM - 1)) == 0).astype(bf16)
        first3 = jnp.concatenate([first_lane] * 3, axis=1)
        g_row = [_dot(first3, jnp.concatenate([head_rows(p) for p in gp], axis=1), NT) for gp in gparts]
        if CW == W:
            g_col = gc
        else:
            spread = (_iota((W, CW), 0) == (_iota((W, CW), 1) >> (C.bit_length() - 1)) * HEAD_DIM).astype(bf16)
            spread3 = jnp.concatenate([spread] * 3, axis=0)
            g_col = [_dot(jnp.concatenate(gp, axis=1), spread3) for gp in gparts]
        yield
        decay = each(lambda c, r: jnp.exp(jnp.minimum(c - r, 0.0)), g_col, g_row)
        k_rows = each(head_rows, k)
        m = each(lambda a, kr, d: jnp.where(ri > si, _dot(a.astype(bf16), kr, NT) * d, 0.0), kb, k_rows, decay)
        a_in = each(lambda a, kr, d: jnp.where(ri >= si, _dot(a.astype(bf16), kr, NT) * d, 0.0).astype(bf16),
                    q, k_rows, decay)
        yield
        pw = [-a for a in m]
        x = pw
        pw_bd = each(head_blocks, pw)
        for _ in range((C - 1).bit_length() - 1):
            pw = each(lambda a, b: _dot(a.astype(bf16), b), pw, pw_bd)
            pw_bd = each(head_blocks, pw)
            x = each(lambda a, p, b: a + p + _dot(a.astype(bf16), b), x, pw, pw_bd)
            yield
        xb = [a.astype(bf16) for a in x]
        u = each(lambda a, b: b + _dot(a, head_rows(b)), xb, vb)
        w = each(lambda a, b: (b + _dot(a, head_rows(b))).astype(bf16), xb, kbe)
        yield
        qe = each(lambda a, b: (a * b).astype(bf16), q, eg)
        kd = each(lambda a, gl, g: (a * jnp.exp(gl - g)).astype(bf16), k, g_last, gc)
        return list(zip(u, w, a_in, qe, kd, each(jnp.exp, g_last)))

    def finish(gen):
        try:
            while True:
                next(gen)
        except StopIteration as stop:
            return stop.value

    def part_b(off, C, u, wb, a_in, qe, kd, eg_last):
        S = S_s[...]
        Sb = S.astype(bf16)
        v_new = u - _dot(wb, Sb)
        vnb = v_new.astype(bf16)
        v_rows = jnp.concatenate([jnp.where(hm[h], vnb, jnp.zeros_like(vnb)) for h in range(N_HEADS)], axis=0)
        o_s[pl.ds(off, C), :] = _dot(qe, Sb) + _dot(a_in, v_rows)
        S_s[...] = S * eg_last + jnp.where(bd_mask, _dot(kd, vnb, TN), 0.0)

    C0 = L % DN_CHUNK
    if C0:
        part_b(0, C0, *finish(part_a([0], C0))[0])

    n_blocks = L // DN_CHUNK
    group = max(g for g in (DN_GROUP, 4, 2, 1) if n_blocks % g == 0)

    def group_body(c, carry):
        offs = [pl.multiple_of(C0 + (c * group + i) * DN_CHUNK, 16) for i in range(group)]
        for off, a in zip(offs, finish(part_a(offs, DN_CHUNK))):
            part_b(off, DN_CHUNK, *a)
        return carry

    lax.fori_loop(0, n_blocks // group, group_body, 0)
    sout_ref[0] = S_s[...]

    for t in range(L // R1):
        rows = pl.ds(t * R1, R1)
        o = o_s[rows, :]
        gate = gate_ref[0, rows, :]
        y = o * lax.rsqrt(head_sum(o * o) * (1.0 / HEAD_DIM) + RMS_EPS) * ng_ref[...]
        yd_ref[0, rows, :] = (y * (gate * jax.nn.sigmoid(gate))).astype(bf16)


def _dn(qkv, gate, small, conv_st, s0_bd, cw, alog_row, dtb_row, ng_row):
    B, L, _ = qkv.shape
    W = GROUP_W
    R1 = _row_tile(L, 768)
    per_b = lambda shape: pl.BlockSpec((1,) + shape, lambda b: (b,) + (0,) * len(shape))
    return pl.pallas_call(
        functools.partial(_dn_kernel, L, R1),
        grid=(B,),
        in_specs=[per_b((L, 3 * W)), per_b((L, W)), per_b((L, SMALL_W)), per_b((3, 3 * W)), per_b((W, W)),
                  _const_spec((4, 3 * W)), _const_spec((1, SMALL_W)), _const_spec((1, SMALL_W)),
                  _const_spec((1, W))],
        out_specs=[per_b((L, W)), per_b((3, 3 * W)), per_b((W, W))],
        out_shape=[jax.ShapeDtypeStruct((B, L, W), bf16), jax.ShapeDtypeStruct((B, 3, 3 * W), f32),
                   jax.ShapeDtypeStruct((B, W, W), f32)],
        scratch_shapes=[pltpu.VMEM((L + 8, 3 * W), f32)] + [pltpu.VMEM((L, W), f32)] * 6
                       + [pltpu.VMEM((W, W), f32)],
        compiler_params=_params("parallel"),
        name="dn",
    )(qkv, gate, small, conv_st, s0_bd, cw, alog_row, dtb_row, ng_row)


def _merge_kernel(alpha, x_ref, ya_ref, yb_ref, yc_ref, yd_ref, wo_ref, g_ref, b_ref, o_ref):
    W = GROUP_W
    m = _dot(ya_ref[...], wo_ref[0:W, :])
    m = m + _dot(yb_ref[...], wo_ref[W:2 * W, :])
    m = m + _dot(yc_ref[...], wo_ref[2 * W:3 * W, :])
    m = m + _dot(yd_ref[...], wo_ref[3 * W:4 * W, :])
    o_ref[...] = _layer_norm(alpha * x_ref[...] + m, g_ref[...], b_ref[...])


def _merge(alpha, x2d, ya, yb, yc, yd, wo, g, b, layer):
    n = x2d.shape[0]
    tm = _row_tile(n, 768)
    row = lambda wd: pl.BlockSpec((tm, wd), lambda i: (i, 0))
    return pl.pallas_call(
        functools.partial(_merge_kernel, alpha),
        grid=(n // tm,),
        in_specs=[row(D_MODEL), row(GROUP_W), row(GROUP_W), row(GROUP_W), row(GROUP_W),
                  _layer_spec((D_MODEL, D_MODEL), layer), _const_spec((1, D_MODEL)), _const_spec((1, D_MODEL))],
        out_specs=row(D_MODEL),
        out_shape=jax.ShapeDtypeStruct((n, D_MODEL), f32),
        compiler_params=_params("parallel"),
        name="merge",
    )(x2d, ya, yb, yc, yd, wo, g, b)


def _ffn_kernel(alpha, Bt, Lt, F, n_chunks, x_ref, st_ref, win_ref, cw_ref, cb_ref, wout_ref, g_ref, b_ref,
                y_ref, nst_ref, carry_s):
    R = Bt * Lt
    Fc = F // n_chunks

    @pl.when(pl.program_id(1) == 0)
    def _():
        carry_s[...] = st_ref[...]

    if Bt == 1:
        x = x_ref[0]
    else:
        x = jnp.concatenate([x_ref[s] for s in range(Bt)], axis=0)
    xb = x.astype(bf16)
    row_in = _iota((R, 1), 0)
    if Bt > 1:
        assert Lt & (Lt - 1) == 0
        row_in = row_in & (Lt - 1)
    first = row_in == 0
    second = row_in == 1

    def rows_of(prev_row, c0):
        parts = [jnp.broadcast_to(carry_s[s, prev_row:prev_row + 1, c0:c0 + Fc], (Lt, Fc)) for s in range(Bt)]
        return parts[0] if Bt == 1 else jnp.concatenate(parts, axis=0)

    acc = jnp.zeros((R, D_MODEL), f32)
    for c in range(n_chunks):
        c0 = c * Fc
        gt = _dot(xb, win_ref[:, c0:c0 + Fc])
        up = _dot(xb, win_ref[:, F + c0:F + c0 + Fc])
        p0 = rows_of(0, c0)
        p1 = rows_of(1, c0)
        g1 = jnp.where(first, p1, pltpu.roll(gt, 1, 0))
        g2 = jnp.where(first, p0, jnp.where(second, p1, pltpu.roll(gt, 2, 0)))
        conv = (cw_ref[0:1, c0:c0 + Fc] * g2 + cw_ref[1:2, c0:c0 + Fc] * g1
                + cw_ref[2:3, c0:c0 + Fc] * gt + cb_ref[:, c0:c0 + Fc])
        hid = (jax.nn.gelu(conv) * up).astype(bf16)
        acc = acc + _dot(hid, wout_ref[c0:c0 + Fc, :])
        for s in range(Bt):
            carry_s[s, :, c0:c0 + Fc] = gt[s * Lt + Lt - 2:s * Lt + Lt, :]
    y = _layer_norm(alpha * x + acc, g_ref[...], b_ref[...])
    for s in range(Bt):
        y_ref[s] = y[s * Lt:(s + 1) * Lt, :]
    nst_ref[...] = carry_s[...]


def _ffn(alpha, x3d, st, win, cw, cb, wout, g, b, layer):
    B, L, _ = x3d.shape
    F = wout.shape[1]
    if L >= 256:
        Bt, Lt = 1, _row_tile(L, 512, 8)
    else:
        Bt, Lt = B, L
    n_chunks = 2
    blk = lambda shape: pl.BlockSpec((Bt,) + shape, lambda i, t: (i, t) + (0,) * (len(shape) - 1))
    st_spec = pl.BlockSpec((Bt, 2, F), lambda i, t: (i, 0, 0))
    return pl.pallas_call(
        functools.partial(_ffn_kernel, alpha, Bt, Lt, F, n_chunks),
        grid=(B // Bt, L // Lt),
        in_specs=[blk((Lt, D_MODEL)), st_spec, _layer_spec((D_MODEL, 2 * F), layer), _const_spec((3, F)),
                  _const_spec((1, F)), _layer_spec((F, D_MODEL), layer), _const_spec((1, D_MODEL)),
                  _const_spec((1, D_MODEL))],
        out_specs=[blk((Lt, D_MODEL)), st_spec],
        out_shape=[jax.ShapeDtypeStruct((B, L, D_MODEL), f32), jax.ShapeDtypeStruct((B, 2, F), f32)],
        scratch_shapes=[pltpu.VMEM((Bt, 2, F), f32)],
        compiler_params=_params("parallel", "arbitrary"),
        name="ffn",
    )(x3d, st, win, cw, cb, wout, g, b)


def _prep_layer_weights(l, P):
    W = GROUP_W
    b_in = P['b_in'][l]
    o_fox = 2 * W
    o_ff = o_fox + 3 * W
    o_sb = o_ff + N_HEADS
    o_dn = o_sb + 3 * W
    o_da = o_dn + 3 * W
    o_db = o_da + N_HEADS
    o_dg = o_db + N_HEADS
    pad = SMALL_W - 3 * N_HEADS

    def reorder(a):
        parts = [a[0:o_ff], a[o_sb:o_da], a[o_dg:o_dg + W], a[o_ff:o_ff + N_HEADS],
                 a[o_da:o_da + 2 * N_HEADS], jnp.zeros((pad,) + a.shape[1:], a.dtype)]
        return jnp.concatenate(parts, axis=0)

    if 'w_in_t' not in P:
        P['w_in_t'] = reorder(jnp.transpose(P['w_in'], (2, 0, 1))).astype(bf16)
        P['w_o_bf'] = P['w_o'].astype(bf16)
        P['ffn_win_bf'] = P['ffn_w_in'].astype(bf16)
        P['ffn_wout_bf'] = P['ffn_w_out'].astype(bf16)
    w_r = P['w_in_t'][:, l, :]
    b_r = reorder(b_in).reshape(1, -1)

    def block_diag(w4):
        eye = jnp.eye(N_HEADS, dtype=w4.dtype)
        return (w4[:, :, None, :] * eye[:, None, :, None]).reshape(W, W)

    small_row = lambda vals, off: jnp.zeros((1, SMALL_W), f32).at[0, off:off + N_HEADS].set(vals)
    r1 = lambda a: a.reshape(1, -1)
    return dict(
        w_in=w_r, b_in=b_r,
        lru_cw=P['lru_conv_w'][l], lru_cb=r1(P['lru_conv_b'][l]),
        lru_wa=block_diag(P['lru_w_a'][l]).astype(bf16), lru_wx=block_diag(P['lru_w_x'][l]).astype(bf16),
        lru_ba=r1(P['lru_b_a'][l]), lru_bx=r1(P['lru_b_x'][l]), lru_lam=r1(P['lru_lambda'][l]),
        gains=[r1(P['grp_norm_g'][l][i]) for i in range(3)],
        dn_cw=P['dn_conv_w'][l], dn_alog=small_row(P['dn_a_log'][l], N_HEADS),
        dn_dtb=small_row(P['dn_dt_bias'][l], N_HEADS), dn_ng=r1(jnp.tile(P['dn_norm_g'][l], N_HEADS)),
        w_o=P['w_o_bf'], ln1_g=r1(P['ln1_g'][l]), ln1_b=r1(P['ln1_b'][l]),
        ffn_win=P['ffn_win_bf'], ffn_cw=P['ffn_conv_w'][l], ffn_cb=r1(P['ffn_conv_b'][l]),
        ffn_wout=P['ffn_wout_bf'], ln2_g=r1(P['ln2_g'][l]), ln2_b=r1(P['ln2_b'][l]),
    )


def _embed_block_diag(S):
    B = S.shape[0]
    eye = jnp.eye(N_HEADS, dtype=S.dtype)
    return (S[:, :, :, None, :] * eye[None, :, None, :, None]).reshape(B, GROUP_W, GROUP_W)


def _extract_block_diag(Sbd):
    return jnp.stack([Sbd[:, h * HEAD_DIM:(h + 1) * HEAD_DIM, h * HEAD_DIM:(h + 1) * HEAD_DIM]
                      for h in range(N_HEADS)], axis=1)


def _trunk(x, st, LW, ln_in, alpha):
    B, L, D = x.shape
    W = GROUP_W
    F = LW[0]['ffn_wout'].shape[1]
    new = {n: [] for n in ('lru_conv', 'lru_h', 'fox_k', 'fox_v', 'fox_logf', 'sb_k', 'sb_v',
                           'dn_conv', 'dn_S', 'ffn_conv')}
    x2 = x.reshape(B * L, D)
    for l, w in enumerate(LW):
        outs = _proj(x2, ln_in[0], ln_in[1], w['w_in'], w['b_in'], do_ln=(l == 0))
        if l == 0:
            x2, outs = outs[0], outs[1:]
        lru, fq, fk, fv, sq, sk, sv, dqkv, dgate, small = [o.reshape(B, L, -1) for o in outs]
        if st is None:
            lru_conv = jnp.zeros((B, 3, W), f32)
            lru_h = jnp.zeros((B, 1, W), f32)
            dn_conv = jnp.zeros((B, 3, 3 * W), f32)
            dn_S = jnp.zeros((B, W, W), f32)
            ffn_conv = jnp.zeros((B, 2, F), f32)
            pfk = pfv = plf = psk = psv = None
        else:
            Pn = st['fox_k'].shape[2]
            lru_conv = st['lru_conv'][l]
            lru_h = st['lru_h'][l].reshape(B, 1, W)
            dn_conv = st['dn_conv'][l]
            dn_S = _embed_block_diag(st['dn_S'][l])
            ffn_conv = st['ffn_conv'][l]
            feat_major = lambda c: jnp.transpose(c.reshape(len(LW), B, Pn, W), (0, 1, 3, 2))
            pfk = feat_major(st['fox_k'])
            pfv = feat_major(st['fox_v'])
            plf = jnp.pad(st['fox_logf'][l], ((0, 0), (0, 0), (0, SMALL_W - N_HEADS)))
            psk = feat_major(st['sb_k'])
            psv = feat_major(st['sb_v'])

        ya, lru_conv_new, h_last = _lru(lru, lru_conv, lru_h, w['lru_cw'], w['lru_cb'], w['lru_wa'],
                                        w['lru_wx'], w['lru_ba'], w['lru_bx'], w['lru_lam'], w['gains'][0])
        kv_t = L >= ATT_SUB
        yb, logf, *fox_kv = _fox(fq, fk, fv, small, pfk, pfv, plf, w['gains'][1], kv_t, l)
        yc, *sb_kv = _sb(sq, sk, sv, psk, psv, w['gains'][2], kv_t, l)
        yd, dn_conv_new, S_new = _dn(dqkv, dgate, small, dn_conv, dn_S, w['dn_cw'], w['dn_alog'],
                                     w['dn_dtb'], w['dn_ng'])
        x1 = _merge(alpha, x2, ya.reshape(B * L, W), yb.reshape(B * L, W), yc.reshape(B * L, W),
                    yd.reshape(B * L, W), w['w_o'], w['ln1_g'], w['ln1_b'], l)
        x3, ffn_new = _ffn(alpha, x1.reshape(B, L, D), ffn_conv, w['ffn_win'], w['ffn_cw'], w['ffn_cb'],
                           w['ffn_wout'], w['ln2_g'], w['ln2_b'], l)
        x2 = x3.reshape(B * L, D)

        if kv_t:
            heads = lambda a: jnp.transpose(a.reshape(B, N_HEADS, HEAD_DIM, L), (0, 3, 1, 2))
            kv = fox_kv + sb_kv
        else:
            heads = lambda a: a.reshape(B, L, N_HEADS, HEAD_DIM)
            kv = [fk, fv, sk, sv]
        new['lru_conv'].append(lru_conv_new)
        new['lru_h'].append(h_last.reshape(B, W))
        new['fox_k'].append(heads(kv[0]))
        new['fox_v'].append(heads(kv[1]))
        new['fox_logf'].append(logf[:, :, :N_HEADS])
        new['sb_k'].append(heads(kv[2]))
        new['sb_v'].append(heads(kv[3]))
        new['dn_conv'].append(dn_conv_new)
        new['dn_S'].append(_extract_block_diag(S_new))
        new['ffn_conv'].append(ffn_new)
    return x2.reshape(B, L, D), {n: jnp.stack(v) for n, v in new.items()}


def kernel(x_prompt, x_sample, state_lru_conv, state_lru_h, cache_fox_k, cache_fox_v, cache_fox_logf,
           cache_sb_k, cache_sb_v, state_dn_conv, state_dn_S, state_ffn_conv, meta_tokens, ln_in_g, ln_in_b,
           w_in, b_in, lru_conv_w, lru_conv_b, lru_w_a, lru_b_a, lru_w_x, lru_b_x, lru_lambda, dn_conv_w,
           dn_a_log, dn_dt_bias, dn_norm_g, grp_norm_g, w_o, ln1_g, ln1_b, ffn_w_in, ffn_conv_w, ffn_conv_b,
           ffn_w_out, ln2_g, ln2_b):
    P = dict(w_in=w_in, b_in=b_in, lru_conv_w=lru_conv_w, lru_conv_b=lru_conv_b, lru_w_a=lru_w_a,
             lru_b_a=lru_b_a, lru_w_x=lru_w_x, lru_b_x=lru_b_x, lru_lambda=lru_lambda, dn_conv_w=dn_conv_w,
             dn_a_log=dn_a_log, dn_dt_bias=dn_dt_bias, dn_norm_g=dn_norm_g, grp_norm_g=grp_norm_g, w_o=w_o,
             ln1_g=ln1_g, ln1_b=ln1_b, ffn_w_in=ffn_w_in, ffn_conv_w=ffn_conv_w, ffn_conv_b=ffn_conv_b,
             ffn_w_out=ffn_w_out, ln2_g=ln2_g, ln2_b=ln2_b)
    depth = w_in.shape[0]
    assert x_prompt.shape[2] == D_MODEL
    alpha = (2.0 * depth) ** 0.25
    LW = [_prep_layer_weights(l, P) for l in range(depth)]
    ln_in = (ln_in_g.reshape(1, -1), ln_in_b.reshape(1, -1))

    bp = x_prompt.shape[0]
    meta = jnp.broadcast_to(meta_tokens.astype(x_prompt.dtype), (bp, N_META, D_MODEL))
    xp = jnp.concatenate([meta, x_prompt], axis=1)
    yp, ps = _trunk(xp, None, LW, ln_in, alpha)

    st_in = dict(lru_conv=state_lru_conv, lru_h=state_lru_h, fox_k=cache_fox_k, fox_v=cache_fox_v,
                 fox_logf=cache_fox_logf, sb_k=cache_sb_k, sb_v=cache_sb_v, dn_conv=state_dn_conv,
                 dn_S=state_dn_S, ffn_conv=state_ffn_conv)
    ys, ss = _trunk(x_sample, st_in, LW, ln_in, alpha)

    names = ('lru_conv', 'lru_h', 'fox_k', 'fox_v', 'fox_logf', 'sb_k', 'sb_v', 'dn_conv', 'dn_S', 'ffn_conv')
    return (yp[:, N_META:], ys) + tuple(ps[n] for n in names) + tuple(ss[n] for n in names)
```

```python
import functools

import jax
import jax.numpy as jnp
from jax import lax
from jax.experimental import pallas as pl
from jax.experimental.pallas import tpu as pltpu

f32 = jnp.float32
bf16 = jnp.bfloat16

D_MODEL = 1024
GROUP_W = 256
N_HEADS = 4
HEAD_DIM = 64
N_META = 16
LRU_C = 8.0
LN_EPS = 1e-5
RMS_EPS = 1e-6
ATT_TILE = 256
ATT_SUB = 128
DN_CHUNK = 64
DN_GROUP = 8
NEG = -1e30
LOG2E = 1.4426950408889634
SMALL_W = 128
VMEM_LIMIT_BYTES = 56 * 1024 * 1024

NN = (((1,), (0,)), ((), ()))
NT = (((1,), (1,)), ((), ()))
TN = (((0,), (0,)), ((), ()))


def _dot(a, b, dims=NN):
    return lax.dot_general(a, b, dims, preferred_element_type=f32)


def _split2(x):
    hi = x.astype(bf16)
    lo = (x - hi.astype(f32)).astype(bf16)
    return hi, lo


def _split3(x):
    p0 = x.astype(bf16)
    r = x - p0.astype(f32)
    p1 = r.astype(bf16)
    p2 = (r - p1.astype(f32)).astype(bf16)
    return p0, p1, p2


def _dot_sel_rhs(x, sel, dims=NN):
    p0, p1, p2 = _split3(x)
    return _dot(p0, sel, dims) + _dot(p1, sel, dims) + _dot(p2, sel, dims)


def _dot_sel_lhs(sel, x, dims=NN):
    p0, p1, p2 = _split3(x)
    return _dot(sel, p0, dims) + _dot(sel, p1, dims) + _dot(sel, p2, dims)


def _iota(shape, dim):
    return lax.broadcasted_iota(jnp.int32, shape, dim)


def _head_masks(width=GROUP_W):
    lane_head = _iota((1, width), 1) >> 6
    return [lane_head == h for h in range(N_HEADS)]


def _softplus(x):
    return jnp.maximum(x, 0.0) + jnp.log1p(jnp.exp(-jnp.abs(x)))


def _log_sigmoid(x):
    return jnp.minimum(x, 0.0) - jnp.log1p(jnp.exp(-jnp.abs(x)))


def _layer_norm(x, g, b):
    mu = jnp.mean(x, -1, keepdims=True)
    xc = x - mu
    var = jnp.mean(xc * xc, -1, keepdims=True)
    return xc * lax.rsqrt(var + LN_EPS) * g + b


def _rms_norm(x, g):
    return x * lax.rsqrt(jnp.mean(x * x, -1, keepdims=True) + RMS_EPS) * g


def _row_tile(n, cap, mult=16):
    best = None
    for d in range(mult, min(n, cap) + 1, mult):
        if n % d == 0:
            best = d
    return best if best is not None else n


def _const_spec(shape):
    nd = len(shape)
    return pl.BlockSpec(shape, lambda *_: (0,) * nd, pipeline_mode=pl.Buffered(1))


def _layer_spec(shape, layer):
    nd = len(shape)
    return pl.BlockSpec((None,) + shape, lambda *_: (layer,) + (0,) * nd, pipeline_mode=pl.Buffered(1))


def _params(*sem):
    return pltpu.CompilerParams(dimension_semantics=sem, vmem_limit_bytes=VMEM_LIMIT_BYTES)


PROJ_WIDTHS = (512, 256, 256, 256, 256, 256, 256, 768, 256, SMALL_W)


def _proj_kernel(do_ln, x_ref, g_ref, b_ref, w_ref, bias_ref, *outs):
    x = x_ref[...]
    if do_ln:
        x = _layer_norm(x, g_ref[...], b_ref[...])
        outs[0][...] = x
        outs = outs[1:]
    xb = x.astype(bf16)
    col = 0
    for o, wd in zip(outs, PROJ_WIDTHS):
        o[...] = _dot(xb, w_ref[col:col + wd, :], NT) + bias_ref[:, col:col + wd]
        col += wd


def _proj(x2d, ln_g, ln_b, w, bias, do_ln):
    n = x2d.shape[0]
    tm = _row_tile(n, 768)
    wtot = w.shape[0]
    row = lambda wd: pl.BlockSpec((tm, wd), lambda i: (i, 0))
    out_shape = [jax.ShapeDtypeStruct((n, wd), f32) for wd in PROJ_WIDTHS]
    out_specs = [row(wd) for wd in PROJ_WIDTHS]
    if do_ln:
        out_shape = [jax.ShapeDtypeStruct((n, D_MODEL), f32)] + out_shape
        out_specs = [row(D_MODEL)] + out_specs
    return pl.pallas_call(
        functools.partial(_proj_kernel, do_ln),
        grid=(n // tm,),
        in_specs=[row(D_MODEL), _const_spec((1, D_MODEL)), _const_spec((1, D_MODEL)),
                  _const_spec((wtot, D_MODEL)), _const_spec((1, wtot))],
        out_specs=out_specs,
        out_shape=out_shape,
        compiler_params=_params("parallel"),
        name="proj",
    )(x2d, ln_g, ln_b, w, bias)


def _lru_kernel(L, R1, lru_ref, cst_ref, h0_ref, cw_ref, cb_ref, wa_ref, wx_ref, ba_ref, bx_ref,
                lam_ref, gain_ref, ya_ref, cnew_ref, hlast_ref, xpad_s, a_s, b_s):
    W = GROUP_W
    xpad_s[pl.ds(0, 8), :] = jnp.zeros((8, W), f32)
    xpad_s[pl.ds(5, 3), :] = cst_ref[0]
    xpad_s[pl.ds(8, L), :] = lru_ref[0, :, 0:W]
    cnew_ref[0] = xpad_s[pl.ds(L + 5, 3), :]

    sp = _softplus(-lam_ref[...])
    cw = cw_ref[...]
    for t in range(L // R1):
        r0 = t * R1
        xa = cb_ref[...]
        for i in range(4):
            xa = xa + cw[i:i + 1, :] * xpad_s[pl.ds(5 + r0 + i, R1), :]
        xb = xa.astype(bf16)
        r = jax.nn.sigmoid(_dot(xb, wa_ref[...]) + ba_ref[...])
        ig = jax.nn.sigmoid(_dot(xb, wx_ref[...]) + bx_ref[...])
        log_a = (-LRU_C) * r * sp
        y2 = 2.0 * log_a
        one_m_a2 = jnp.tanh(-0.5 * y2) * (1.0 + jnp.exp(y2))
        a_s[pl.ds(r0, R1), :] = jnp.exp(log_a)
        b_s[pl.ds(r0, R1), :] = jnp.sqrt(one_m_a2) * (ig * xa)

    rows8 = _iota((8, W), 0)

    def scan_body(g, hprev):
        off = pl.multiple_of(g * 8, 8)
        A = a_s[pl.ds(off, 8), :]
        Bv = b_s[pl.ds(off, 8), :]
        for s in (1, 2, 4):
            keep = rows8 >= s
            a_sh = jnp.where(keep, pltpu.roll(A, s, 0), 1.0)
            b_sh = jnp.where(keep, pltpu.roll(Bv, s, 0), 0.0)
            Bv = A * b_sh + Bv
            A = A * a_sh
        h = A * hprev + Bv
        b_s[pl.ds(off, 8), :] = h
        return h[7:8, :]

    hlast_ref[0] = lax.fori_loop(0, L // 8, scan_body, h0_ref[0])

    for t in range(L // R1):
        r0 = t * R1
        o = jax.nn.gelu(lru_ref[0, pl.ds(r0, R1), W:2 * W]) * b_s[pl.ds(r0, R1), :]
        ya_ref[0, pl.ds(r0, R1), :] = _rms_norm(o, gain_ref[...]).astype(bf16)


def _lru(lru3d, conv_st, h0, cw, cb, wa, wx, ba, bx, lam, gain):
    B, L, _ = lru3d.shape
    R1 = _row_tile(L, 768)
    W = GROUP_W
    per_b = lambda shape: pl.BlockSpec((1,) + shape, lambda b: (b,) + (0,) * len(shape))
    return pl.pallas_call(
        functools.partial(_lru_kernel, L, R1),
        grid=(B,),
        in_specs=[per_b((L, 2 * W)), per_b((3, W)), per_b((1, W)),
                  _const_spec((4, W)), _const_spec((1, W)), _const_spec((W, W)), _const_spec((W, W)),
                  _const_spec((1, W)), _const_spec((1, W)), _const_spec((1, W)), _const_spec((1, W))],
        out_specs=[per_b((L, W)), per_b((3, W)), per_b((1, W))],
        out_shape=[jax.ShapeDtypeStruct((B, L, W), bf16), jax.ShapeDtypeStruct((B, 3, W), f32),
                   jax.ShapeDtypeStruct((B, 1, W), f32)],
        scratch_shapes=[pltpu.VMEM((L + 8, W), f32), pltpu.VMEM((L, W), f32), pltpu.VMEM((L, W), f32)],
        compiler_params=_params("parallel"),
        name="lru",
    )(lru3d, conv_st, h0, cw, cb, wa, wx, ba, bx, lam, gain)


def _att_layout(L, P):
    T = ATT_TILE
    assert (P + L) % 16 == 0
    front = (-(P + L)) % T
    Lpad = front + P + L
    L0 = L % T
    nT = L // T
    base_t = (front + P + L0) // T
    return front, Lpad, L0, nT, base_t


def _stage_pairs(front, P, L, new_ref, past_ref, flat_s, pair_s, hm):
    W = GROUP_W
    S = ATT_SUB
    if front:
        flat_s[pl.ds(0, front), :] = jnp.zeros((front, W), bf16)
    if P:
        assert P % S == 0

        def past_body(u, c):
            r = pl.multiple_of(u * S, S)
            flat_s[pl.ds(pl.multiple_of(front + r, 16), S), :] = past_ref[0, :, pl.ds(r, S)].T.astype(bf16)
            return c

        lax.fori_loop(0, P // S, past_body, 0)
    flat_s[pl.ds(front + P, L), :] = new_ref[0].astype(bf16)
    zero = jnp.zeros((S, W), bf16)

    def body(u, c):
        x = flat_s[pl.ds(pl.multiple_of(u * S, S), S), :]
        base = pl.multiple_of(u * 2 * S, 2 * S)
        for p in range(2):
            pair_s[p, pl.ds(base, S), :] = jnp.where(hm[2 * p], x, zero)
            pair_s[p, pl.ds(base + S, S), :] = jnp.where(hm[2 * p + 1], x, zero)
        return c

    lax.fori_loop(0, (front + P + L) // S, body, 0)


def _store_transposed(src_ref, dst_ref, L):
    S = ATT_SUB
    n_full = L // S

    def body(u, c):
        r = pl.multiple_of(u * S, S)
        dst_ref[0, :, pl.ds(r, S)] = src_ref[0, pl.ds(r, S), :].T
        return c

    lax.fori_loop(0, n_full, body, 0)
    tail = L - n_full * S
    if tail:
        eye = (_iota((GROUP_W, GROUP_W), 0) == _iota((GROUP_W, GROUP_W), 1)).astype(bf16)
        dst_ref[0, :, pl.ds(n_full * S, tail)] = _dot_sel_lhs(eye, src_ref[0, pl.ds(n_full * S, tail), :], NT)


def _pair_scores(qb, kpair_s, j):
    S2 = 2 * ATT_SUB
    kbase = j * (2 * ATT_TILE)
    return [[_dot(qb, kpair_s[p, pl.ds(pl.multiple_of(kbase + sub * S2, S2), S2), :], NT)
             for sub in range(2)] for p in range(2)]


def _pair_pv(wb, vpair_s, j):
    S2 = 2 * ATT_SUB
    kbase = j * (2 * ATT_TILE)
    out = None
    for sub in range(2):
        for p in range(2):
            lhs = jnp.concatenate([wb[2 * p][sub], wb[2 * p + 1][sub]], axis=1)
            t = _dot(lhs, vpair_s[p, pl.ds(pl.multiple_of(kbase + sub * S2, S2), S2), :])
            out = t if out is None else out + t
    return out


def _by_head_lanes(vals, lo_half):
    return jnp.concatenate([jnp.where(lo_half, vals[0], vals[1]), jnp.where(lo_half, vals[2], vals[3])], axis=1)


def _fox_kernel(L, P, emit_t, *refs):
    T, S = ATT_TILE, ATT_SUB
    front, Lpad, L0, nT, base_t = _att_layout(L, P)
    n_in = 8 if P else 5
    n_out = 4 if emit_t else 2
    if P:
        q_ref, k_ref, v_ref, sm_ref, pk_ref, pv_ref, plf_ref, gain_ref = refs[:n_in]
    else:
        q_ref, k_ref, v_ref, sm_ref, gain_ref = refs[:n_in]
        pk_ref = pv_ref = plf_ref = None
    yb_ref, lf_ref = refs[n_in:n_in + 2]
    scr = refs[n_in + n_out:]
    flat_s, kpair_s, vpair_s, lfp_s, F_s, FT_s, qb_s, fqb_s, m_s, lp_s, acc_s, cb_s = scr
    hm = _head_masks()
    lo_half = _iota((1, S), 1) < HEAD_DIM
    scale = HEAD_DIM ** -0.5 * LOG2E
    if emit_t:
        _store_transposed(k_ref, refs[n_in + 2], L)
        _store_transposed(v_ref, refs[n_in + 3], L)

    _stage_pairs(front, P, L, k_ref, pk_ref, flat_s, kpair_s, hm)
    _stage_pairs(front, P, L, v_ref, pv_ref, flat_s, vpair_s, hm)

    if front:
        lfp_s[pl.ds(0, front), :] = jnp.zeros((front, SMALL_W), f32)
    if P:
        lfp_s[pl.ds(front, P), :] = plf_ref[0]
    lf = _log_sigmoid(sm_ref[0])
    lf_ref[0] = lf
    lfp_s[pl.ds(front + P, L), :] = lf
    ltri = (_iota((S, S), 0) >= _iota((S, S), 1)).astype(bf16)
    sel8 = (_iota((8, SMALL_W), 0) == _iota((8, SMALL_W), 1)).astype(bf16)

    n_ct = Lpad // S
    local = [_dot_sel_lhs(ltri, lfp_s[pl.ds(u * S, S), :]) for u in range(n_ct)]
    offset = jnp.zeros((1, SMALL_W), f32)
    F2 = []
    for u in range(n_ct):
        F2.append((local[u] + offset) * LOG2E)
        offset = offset + local[u][S - 1:S, :]
    for u in range(n_ct):
        F_s[pl.ds(u * S, S), :] = F2[u]
    for u in range(n_ct):
        FT_s[:, pl.ds(u * S, S)] = _dot_sel_lhs(sel8, F2[u], NT)
    if front:
        FT_s[:, pl.ds(0, front)] = jnp.full((8, front), -NEG, f32)
    cb_s[...] = jnp.where(_iota((T, T), 1) <= _iota((T, T), 0), 0.0, NEG)

    def process(qnat, tq, d):
        rows = pl.ds(0, tq)
        qp0 = front + P + qnat
        qb_s[rows, :] = (q_ref[0, pl.ds(qnat, tq), :] * scale).astype(bf16)
        Fq = F_s[pl.ds(qp0, tq), :]
        for h in range(N_HEADS):
            fqb_s[h, rows, :] = jnp.broadcast_to(Fq[:, h:h + 1], (tq, S))
            m_s[h, rows, :] = jnp.full((tq, S), NEG, f32)
            lp_s[h, rows, :] = jnp.zeros((tq, S), f32)
        acc_s[rows, :] = jnp.zeros((tq, GROUP_W), f32)

        def scores(j):
            return _pair_scores(qb_s[rows, :], kpair_s, j)

        def update(j, sc, mode):
            if mode == 'ragged':
                qpos = qp0 + _iota((tq, S), 0)
                vis = []
                for sub in range(2):
                    kpos = j * T + sub * S + _iota((tq, S), 1)
                    ok = kpos <= qpos
                    if front:
                        ok = ok & (kpos >= front)
                    vis.append(ok)
            alphas, pb = [], []
            for h in range(N_HEADS):
                p_, half = divmod(h, 2)
                fq = fqb_s[h, rows, :]
                c = []
                for sub in range(2):
                    fk = FT_s[h:h + 1, pl.ds(pl.multiple_of(j * T + sub * S, S), S)]
                    x = sc[p_][sub][:, half * S:(half + 1) * S] + (fq - fk)
                    if mode == 'ragged':
                        x = jnp.where(vis[sub], x, NEG)
                    elif mode == 'diag':
                        x = x + cb_s[:, sub * S:(sub + 1) * S]
                    c.append(x)
                m_old = m_s[h, rows, :]
                m_new = jnp.maximum(m_old, jnp.max(jnp.maximum(c[0], c[1]), -1, keepdims=True))
                alpha = jnp.exp2(m_old - m_new)
                e0 = jnp.exp2(c[0] - m_new)
                e1 = jnp.exp2(c[1] - m_new)
                lp_s[h, rows, :] = alpha * lp_s[h, rows, :] + (e0 + e1)
                m_s[h, rows, :] = m_new
                alphas.append(alpha)
                pb.append((e0.astype(bf16), e1.astype(bf16)))
            pv = _pair_pv(pb, vpair_s, j)
            acc_s[rows, :] = acc_s[rows, :] * _by_head_lanes(alphas, lo_half) + pv

        def pair(j0, mode0, j1, mode1):
            sc0 = scores(j0)
            sc1 = scores(j1)
            update(j0, sc0, mode0)
            update(j1, sc1, mode1)

        def plain2(t, c):
            pair(2 * t, 'plain', 2 * t + 1, 'plain')
            return c

        lax.fori_loop(0, d // 2, plain2, 0)
        if isinstance(d, int):
            if d % 2:
                pair(d - 1, 'plain', d, 'ragged')
            else:
                update(d, scores(d), 'ragged')
        else:
            @pl.when(d % 2 == 1)
            def _():
                pair(d - 1, 'plain', d, 'diag')

            @pl.when(d % 2 == 0)
            def _():
                update(d, scores(d), 'diag')

        l = [jnp.sum(lp_s[h, rows, :], -1, keepdims=True) for h in range(N_HEADS)]
        out = acc_s[rows, :] / _by_head_lanes(l, lo_half)
        yb_ref[0, pl.ds(qnat, tq), :] = _rms_norm(out, gain_ref[...]).astype(bf16)

    if L0:
        process(0, L0, base_t - 1)

    def q_body(i, c):
        process(pl.multiple_of(L0 + i * T, 16), T, base_t + i)
        return c

    lax.fori_loop(0, nT, q_body, 0)


def _fox(q, k, v, small, past_k, past_v, past_lf, gain, emit_t, layer):
    B, L, W = q.shape
    P = 0 if past_k is None else past_k.shape[3]
    T = ATT_TILE
    _, Lpad, _, _, _ = _att_layout(L, P)
    per_b = lambda shape: pl.BlockSpec((1,) + shape, lambda b: (b,) + (0,) * len(shape))
    per_lb = lambda shape: pl.BlockSpec((None, 1) + shape, lambda b: (layer, b) + (0,) * len(shape))
    in_specs = [per_b((L, W)), per_b((L, W)), per_b((L, W)), per_b((L, SMALL_W))]
    args = [q, k, v, small]
    if P:
        in_specs += [per_lb((W, P)), per_lb((W, P)), per_b((P, SMALL_W))]
        args += [past_k, past_v, past_lf]
    in_specs.append(_const_spec((1, W)))
    args.append(gain)
    out_specs = [per_b((L, W)), per_b((L, SMALL_W))]
    out_shape = [jax.ShapeDtypeStruct((B, L, W), bf16), jax.ShapeDtypeStruct((B, L, SMALL_W), f32)]
    if emit_t:
        out_specs += [per_b((W, L)), per_b((W, L))]
        out_shape += [jax.ShapeDtypeStruct((B, W, L), f32)] * 2
    return pl.pallas_call(
        functools.partial(_fox_kernel, L, P, emit_t),
        grid=(B,),
        in_specs=in_specs,
        out_specs=out_specs,
        out_shape=out_shape,
        scratch_shapes=[pltpu.VMEM((Lpad, W), bf16), pltpu.VMEM((2, 2 * Lpad, W), bf16),
                        pltpu.VMEM((2, 2 * Lpad, W), bf16), pltpu.VMEM((Lpad, SMALL_W), f32),
                        pltpu.VMEM((Lpad, SMALL_W), f32), pltpu.VMEM((8, Lpad), f32),
                        pltpu.VMEM((T, W), bf16), pltpu.VMEM((N_HEADS, T, ATT_SUB), f32),
                        pltpu.VMEM((N_HEADS, T, ATT_SUB), f32), pltpu.VMEM((N_HEADS, T, ATT_SUB), f32),
                        pltpu.VMEM((T, W), f32), pltpu.VMEM((T, T), f32)],
        compiler_params=_params("parallel"),
        name="fox",
    )(*args)


def _sb_kernel(L, P, emit_t, *refs):
    T, S = ATT_TILE, ATT_SUB
    front, Lpad, L0, nT, base_t = _att_layout(L, P)
    n_in = 6 if P else 4
    n_out = 3 if emit_t else 1
    if P:
        q_ref, k_ref, v_ref, pk_ref, pv_ref, gain_ref = refs[:n_in]
    else:
        q_ref, k_ref, v_ref, gain_ref = refs[:n_in]
        pk_ref = pv_ref = None
    yc_ref = refs[n_in]
    scr = refs[n_in + n_out:]
    flat_s, kpair_s, vpair_s, qb_s, c_s, acc_s, cb_s = scr
    hm = _head_masks()
    scale = HEAD_DIM ** -0.5 * LOG2E
    if emit_t:
        _store_transposed(k_ref, refs[n_in + 1], L)
        _store_transposed(v_ref, refs[n_in + 2], L)

    _stage_pairs(front, P, L, k_ref, pk_ref, flat_s, kpair_s, hm)
    _stage_pairs(front, P, L, v_ref, pv_ref, flat_s, vpair_s, hm)

    after2 = ((_iota((2 * T, T), 0) & (T - 1)) >= _iota((2 * T, T), 1)).astype(bf16)
    cb_s[...] = jnp.where(_iota((T, T), 1) < _iota((T, T), 0), 0.0, NEG)
    pad_bias = jnp.where(_iota((1, T), 1) >= front, 0.0, NEG)

    def process(qnat, tq, d):
        rows = pl.ds(0, tq)
        qp0 = front + P + qnat
        qb_s[rows, :] = (q_ref[0, pl.ds(qnat, tq), :] * scale).astype(bf16)
        for h in range(N_HEADS):
            c_s[h, rows, :] = jnp.zeros((tq, S), f32)
        acc_s[rows, :] = jnp.zeros((tq, GROUP_W), f32)

        def scores(j):
            return _pair_scores(qb_s[rows, :], kpair_s, j)

        def update(j, sc, mode):
            if mode == 'ragged':
                kpos = j * T + _iota((tq, T), 1)
                vis = kpos < qp0 + _iota((tq, T), 0)
                if front:
                    vis = vis & (kpos >= front)
            wb = []
            for h in range(N_HEADS):
                p_, half = divmod(h, 2)
                z = jnp.concatenate([sc[p_][sub][:, half * S:(half + 1) * S] for sub in range(2)], axis=1)
                if mode == 'diag':
                    z = z + cb_s[...]
                elif mode == 'pad':
                    z = z + pad_bias
                nk = jnp.maximum(z, 0.0) + jnp.log2(1.0 + jnp.exp2(-jnp.abs(z)))
                if mode == 'ragged':
                    nk = jnp.where(vis, nk, 0.0)
                hi, lo = _split2(nk)
                incl = _dot(jnp.concatenate([hi, lo], axis=1), after2)
                c = c_s[h, rows, :]
                w = jnp.exp2(z - incl - jnp.concatenate([c, c], axis=1))
                if mode == 'ragged':
                    w = jnp.where(vis, w, 0.0)
                c_s[h, rows, :] = c + jnp.sum(nk, -1, keepdims=True)
                wbf = w.astype(bf16)
                wb.append((wbf[:, 0:S], wbf[:, S:2 * S]))
            acc_s[rows, :] = acc_s[rows, :] + _pair_pv(wb, vpair_s, j)

        def kv_step(j, mode):
            update(j, scores(j), mode)

        def plain2(t, c):
            j = d - 1 - 2 * t
            sc0 = scores(j)
            sc1 = scores(j - 1)
            update(j, sc0, 'plain')
            update(j - 1, sc1, 'plain')
            return c

        jf = 1 if front else 0
        kv_step(d, 'ragged' if isinstance(d, int) else 'diag')
        if (not isinstance(d, int)) or d > 0:
            n_plain = d - jf
            lax.fori_loop(0, n_plain // 2, plain2, 0)

            @pl.when(n_plain % 2 == 1)
            def _():
                kv_step(jf, 'plain')
            if front:
                kv_step(0, 'pad')

        yc_ref[0, pl.ds(qnat, tq), :] = _rms_norm(acc_s[rows, :], gain_ref[...]).astype(bf16)

    if L0:
        process(0, L0, base_t - 1)

    def q_body(i, c):
        process(pl.multiple_of(L0 + i * T, 16), T, base_t + i)
        return c

    lax.fori_loop(0, nT, q_body, 0)


def _sb(q, k, v, past_k, past_v, gain, emit_t, layer):
    B, L, W = q.shape
    P = 0 if past_k is None else past_k.shape[3]
    per_lb = lambda shape: pl.BlockSpec((None, 1) + shape, lambda b: (layer, b) + (0,) * len(shape))
    T = ATT_TILE
    _, Lpad, _, _, _ = _att_layout(L, P)
    per_b = lambda shape: pl.BlockSpec((1,) + shape, lambda b: (b,) + (0,) * len(shape))
    in_specs = [per_b((L, W)), per_b((L, W)), per_b((L, W))]
    args = [q, k, v]
    if P:
        in_specs += [per_lb((W, P)), per_lb((W, P))]
        args += [past_k, past_v]
    in_specs.append(_const_spec((1, W)))
    args.append(gain)
    out_specs = [per_b((L, W))]
    out_shape = [jax.ShapeDtypeStruct((B, L, W), bf16)]
    if emit_t:
        out_specs += [per_b((W, L)), per_b((W, L))]
        out_shape += [jax.ShapeDtypeStruct((B, W, L), f32)] * 2
    return pl.pallas_call(
        functools.partial(_sb_kernel, L, P, emit_t),
        grid=(B,),
        in_specs=in_specs,
        out_specs=out_specs,
        out_shape=out_shape,
        scratch_shapes=[pltpu.VMEM((Lpad, W), bf16), pltpu.VMEM((2, 2 * Lpad, W), bf16),
                        pltpu.VMEM((2, 2 * Lpad, W), bf16), pltpu.VMEM((T, W), bf16),
                        pltpu.VMEM((N_HEADS, T, ATT_SUB), f32), pltpu.VMEM((T, W), f32),
                        pltpu.VMEM((T, T), f32)],
        compiler_params=_params("parallel"),
        name="sb",
    )(*args)


def _dn_kernel(L, R1, qkv_ref, gate_ref, sm_ref, cst_ref, s0_ref, cw_ref, alog_ref, dtb_ref, ng_ref,
               yd_ref, cnew_ref, sout_ref, xpad_s, q_s, k_s, v_s, g_s, beta_s, o_s, S_s):
    W = GROUP_W
    hm = _head_masks()
    bd_ones = ((_iota((W, W), 0) >> 6) == (_iota((W, W), 1) >> 6)).astype(bf16)
    exp_g = (_iota((SMALL_W, W), 0) == (_iota((SMALL_W, W), 1) >> 6) + N_HEADS).astype(bf16)
    exp_b = (_iota((SMALL_W, W), 0) == (_iota((SMALL_W, W), 1) >> 6) + 2 * N_HEADS).astype(bf16)

    def head_sum(x):
        hi, lo = _split2(x)
        return _dot(hi, bd_ones) + _dot(lo, bd_ones)

    xpad_s[pl.ds(0, 8), :] = jnp.zeros((8, 3 * W), f32)
    xpad_s[pl.ds(5, 3), :] = cst_ref[0]
    xpad_s[pl.ds(8, L), :] = qkv_ref[0]
    cnew_ref[0] = xpad_s[pl.ds(L + 5, 3), :]
    cw = cw_ref[...]
    neg_a = -jnp.exp(alog_ref[...])
    for t in range(L // R1):
        r0 = t * R1
        rows = pl.ds(r0, R1)
        xc = cw[0:1, :] * xpad_s[pl.ds(5 + r0, R1), :]
        for i in range(1, 4):
            xc = xc + cw[i:i + 1, :] * xpad_s[pl.ds(5 + r0 + i, R1), :]
        xc = xc * jax.nn.sigmoid(xc)
        q = xc[:, 0:W]
        k = xc[:, W:2 * W]
        q_s[rows, :] = q * lax.rsqrt(head_sum(q * q) + RMS_EPS) * (HEAD_DIM ** -0.5)
        k_s[rows, :] = k * lax.rsqrt(head_sum(k * k) + RMS_EPS)
        v_s[rows, :] = xc[:, 2 * W:3 * W]
        sm = sm_ref[0, rows, :]
        g_s[rows, :] = _dot_sel_rhs(neg_a * _softplus(sm + dtb_ref[...]), exp_g)
        beta_s[rows, :] = _dot_sel_rhs(jax.nn.sigmoid(sm), exp_b)

    S_s[...] = s0_ref[0]
    bd_mask = (_iota((W, W), 0) >> 6) == (_iota((W, W), 1) >> 6)

    def part_a(offs, C):
        each = lambda f, *ls: [f(*a) for a in zip(*ls)]
        CW = N_HEADS * C
        assert C & (C - 1) == 0
        ri = _iota((C, CW), 0)
        si = _iota((C, CW), 1) & (C - 1)
        blk = [(_iota((1, CW), 1) >> (C.bit_length() - 1)) == h for h in range(N_HEADS)]
        ltri = (_iota((C, C), 0) >= _iota((C, C), 1)).astype(bf16)

        def head_rows(x):
            xb = x.astype(bf16)
            return jnp.concatenate([jnp.where(hm[h], xb, jnp.zeros_like(xb)) for h in range(N_HEADS)], axis=0)

        def head_blocks(y):
            yb = y.astype(bf16)
            return jnp.concatenate([jnp.where(blk[h], yb, jnp.zeros_like(yb)) for h in range(N_HEADS)], axis=0)

        q = [q_s[pl.ds(off, C), :] for off in offs]
        k = [k_s[pl.ds(off, C), :] for off in offs]
        beta = [beta_s[pl.ds(off, C), :] for off in offs]
        gc = [_dot_sel_lhs(ltri, g_s[pl.ds(off, C), :]) for off in offs]
        eg = each(jnp.exp, gc)
        g_last = [g[C - 1:C, :] for g in gc]
        kb = each(lambda a, b: a * b, k, beta)
        vb = [v_s[pl.ds(off, C), :] * b for off, b in zip(offs, beta)]
        kbe = each(lambda a, b: a * b, kb, eg)

        gparts = each(_split3, gc)
        first_lane = ((_iota((C, W), 1) & (HEAD_DIM - 1)) == 0).astype(bf16)
        first3 = jnp.concatenate([first_lane] * 3, axis=1)
        g_row = [_dot(first3, jnp.concatenate([head_rows(p) for p in gp], axis=1), NT) for gp in gparts]
        if CW == W:
            g_col = gc
        else:
            spread = (_iota((W, CW), 0) == (_iota((W, CW), 1) >> (C.bit_length() - 1)) * HEAD_DIM).astype(bf16)
            spread3 = jnp.concatenate([spread] * 3, axis=0)
            g_col = [_dot(jnp.concatenate(gp, axis=1), spread3) for gp in gparts]
        decay = each(lambda c, r: jnp.exp(jnp.minimum(c - r, 0.0)), g_col, g_row)
        k_rows = each(head_rows, k)
        m = each(lambda a, kr, d: jnp.where(ri > si, _dot(a.astype(bf16), kr, NT) * d, 0.0), kb, k_rows, decay)
        a_in = each(lambda a, kr, d: jnp.where(ri >= si, _dot(a.astype(bf16), kr, NT) * d, 0.0).astype(bf16),
                    q, k_rows, decay)
        pw = [-a for a in m]
        x = pw
        pw_bd = each(head_blocks, pw)
        for _ in range((C - 1).bit_length() - 1):
            pw = each(lambda a, b: _dot(a.astype(bf16), b), pw, pw_bd)
            pw_bd = each(head_blocks, pw)
            x = each(lambda a, p, b: a + p + _dot(a.astype(bf16), b), x, pw, pw_bd)
        xb = [a.astype(bf16) for a in x]
        u = each(lambda a, b: b + _dot(a, head_rows(b)), xb, vb)
        w = each(lambda a, b: (b + _dot(a, head_rows(b))).astype(bf16), xb, kbe)
        qe = each(lambda a, b: (a * b).astype(bf16), q, eg)
        kd = each(lambda a, gl, g: (a * jnp.exp(gl - g)).astype(bf16), k, g_last, gc)
        return list(zip(u, w, a_in, qe, kd, each(jnp.exp, g_last)))

    def part_b(off, C, u, wb, a_in, qe, kd, eg_last):
        S = S_s[...]
        Sb = S.astype(bf16)
        v_new = u - _dot(wb, Sb)
        vnb = v_new.astype(bf16)
        v_rows = jnp.concatenate([jnp.where(hm[h], vnb, jnp.zeros_like(vnb)) for h in range(N_HEADS)], axis=0)
        o_s[pl.ds(off, C), :] = _dot(qe, Sb) + _dot(a_in, v_rows)
        S_s[...] = S * eg_last + jnp.where(bd_mask, _dot(kd, vnb, TN), 0.0)

    C0 = L % DN_CHUNK
    if C0:
        part_b(0, C0, *part_a([0], C0)[0])

    n_blocks = L // DN_CHUNK
    group = max(g for g in (DN_GROUP, 4, 2, 1) if n_blocks % g == 0)

    def group_body(c, carry):
        offs = [pl.multiple_of(C0 + (c * group + i) * DN_CHUNK, 16) for i in range(group)]
        for off, a in zip(offs, part_a(offs, DN_CHUNK)):
            part_b(off, DN_CHUNK, *a)
        return carry

    lax.fori_loop(0, n_blocks // group, group_body, 0)
    sout_ref[0] = S_s[...]

    for t in range(L // R1):
        rows = pl.ds(t * R1, R1)
        o = o_s[rows, :]
        gate = gate_ref[0, rows, :]
        y = o * lax.rsqrt(head_sum(o * o) * (1.0 / HEAD_DIM) + RMS_EPS) * ng_ref[...]
        yd_ref[0, rows, :] = (y * (gate * jax.nn.sigmoid(gate))).astype(bf16)


def _dn(qkv, gate, small, conv_st, s0_bd, cw, alog_row, dtb_row, ng_row):
    B, L, _ = qkv.shape
    W = GROUP_W
    R1 = _row_tile(L, 768)
    per_b = lambda shape: pl.BlockSpec((1,) + shape, lambda b: (b,) + (0,) * len(shape))
    return pl.pallas_call(
        functools.partial(_dn_kernel, L, R1),
        grid=(B,),
        in_specs=[per_b((L, 3 * W)), per_b((L, W)), per_b((L, SMALL_W)), per_b((3, 3 * W)), per_b((W, W)),
                  _const_spec((4, 3 * W)), _const_spec((1, SMALL_W)), _const_spec((1, SMALL_W)),
                  _const_spec((1, W))],
        out_specs=[per_b((L, W)), per_b((3, 3 * W)), per_b((W, W))],
        out_shape=[jax.ShapeDtypeStruct((B, L, W), bf16), jax.ShapeDtypeStruct((B, 3, 3 * W), f32),
                   jax.ShapeDtypeStruct((B, W, W), f32)],
        scratch_shapes=[pltpu.VMEM((L + 8, 3 * W), f32)] + [pltpu.VMEM((L, W), f32)] * 6
                       + [pltpu.VMEM((W, W), f32)],
        compiler_params=_params("parallel"),
        name="dn",
    )(qkv, gate, small, conv_st, s0_bd, cw, alog_row, dtb_row, ng_row)


def _merge_kernel(alpha, x_ref, ya_ref, yb_ref, yc_ref, yd_ref, wo_ref, g_ref, b_ref, o_ref):
    W = GROUP_W
    m = _dot(ya_ref[...], wo_ref[0:W, :])
    m = m + _dot(yb_ref[...], wo_ref[W:2 * W, :])
    m = m + _dot(yc_ref[...], wo_ref[2 * W:3 * W, :])
    m = m + _dot(yd_ref[...], wo_ref[3 * W:4 * W, :])
    o_ref[...] = _layer_norm(alpha * x_ref[...] + m, g_ref[...], b_ref[...])


def _merge(alpha, x2d, ya, yb, yc, yd, wo, g, b, layer):
    n = x2d.shape[0]
    tm = _row_tile(n, 768)
    row = lambda wd: pl.BlockSpec((tm, wd), lambda i: (i, 0))
    return pl.pallas_call(
        functools.partial(_merge_kernel, alpha),
        grid=(n // tm,),
        in_specs=[row(D_MODEL), row(GROUP_W), row(GROUP_W), row(GROUP_W), row(GROUP_W),
                  _layer_spec((D_MODEL, D_MODEL), layer), _const_spec((1, D_MODEL)), _const_spec((1, D_MODEL))],
        out_specs=row(D_MODEL),
        out_shape=jax.ShapeDtypeStruct((n, D_MODEL), f32),
        compiler_params=_params("parallel"),
        name="merge",
    )(x2d, ya, yb, yc, yd, wo, g, b)


def _ffn_kernel(alpha, Bt, Lt, F, n_chunks, x_ref, st_ref, win_ref, cw_ref, cb_ref, wout_ref, g_ref, b_ref,
                y_ref, nst_ref, carry_s):
    R = Bt * Lt
    Fc = F // n_chunks

    @pl.when(pl.program_id(1) == 0)
    def _():
        carry_s[...] = st_ref[...]

    if Bt == 1:
        x = x_ref[0]
    else:
        x = jnp.concatenate([x_ref[s] for s in range(Bt)], axis=0)
    xb = x.astype(bf16)
    row_in = _iota((R, 1), 0)
    if Bt > 1:
        assert Lt & (Lt - 1) == 0
        row_in = row_in & (Lt - 1)
    first = row_in == 0
    second = row_in == 1

    def rows_of(prev_row, c0):
        parts = [jnp.broadcast_to(carry_s[s, prev_row:prev_row + 1, c0:c0 + Fc], (Lt, Fc)) for s in range(Bt)]
        return parts[0] if Bt == 1 else jnp.concatenate(parts, axis=0)

    acc = jnp.zeros((R, D_MODEL), f32)
    for c in range(n_chunks):
        c0 = c * Fc
        gt = _dot(xb, win_ref[:, c0:c0 + Fc])
        up = _dot(xb, win_ref[:, F + c0:F + c0 + Fc])
        p0 = rows_of(0, c0)
        p1 = rows_of(1, c0)
        g1 = jnp.where(first, p1, pltpu.roll(gt, 1, 0))
        g2 = jnp.where(first, p0, jnp.where(second, p1, pltpu.roll(gt, 2, 0)))
        conv = (cw_ref[0:1, c0:c0 + Fc] * g2 + cw_ref[1:2, c0:c0 + Fc] * g1
                + cw_ref[2:3, c0:c0 + Fc] * gt + cb_ref[:, c0:c0 + Fc])
        hid = (jax.nn.gelu(conv) * up).astype(bf16)
        acc = acc + _dot(hid, wout_ref[c0:c0 + Fc, :])
        for s in range(Bt):
            carry_s[s, :, c0:c0 + Fc] = gt[s * Lt + Lt - 2:s * Lt + Lt, :]
    y = _layer_norm(alpha * x + acc, g_ref[...], b_ref[...])
    for s in range(Bt):
        y_ref[s] = y[s * Lt:(s + 1) * Lt, :]
    nst_ref[...] = carry_s[...]


def _ffn(alpha, x3d, st, win, cw, cb, wout, g, b, layer):
    B, L, _ = x3d.shape
    F = wout.shape[1]
    if L >= 256:
        Bt, Lt = 1, _row_tile(L, 512, 8)
    else:
        Bt, Lt = B, L
    n_chunks = 2
    blk = lambda shape: pl.BlockSpec((Bt,) + shape, lambda i, t: (i, t) + (0,) * (len(shape) - 1))
    st_spec = pl.BlockSpec((Bt, 2, F), lambda i, t: (i, 0, 0))
    return pl.pallas_call(
        functools.partial(_ffn_kernel, alpha, Bt, Lt, F, n_chunks),
        grid=(B // Bt, L // Lt),
        in_specs=[blk((Lt, D_MODEL)), st_spec, _layer_spec((D_MODEL, 2 * F), layer), _const_spec((3, F)),
                  _const_spec((1, F)), _layer_spec((F, D_MODEL), layer), _const_spec((1, D_MODEL)),
                  _const_spec((1, D_MODEL))],
        out_specs=[blk((Lt, D_MODEL)), st_spec],
        out_shape=[jax.ShapeDtypeStruct((B, L, D_MODEL), f32), jax.ShapeDtypeStruct((B, 2, F), f32)],
        scratch_shapes=[pltpu.VMEM((Bt, 2, F), f32)],
        compiler_params=_params("parallel", "arbitrary"),
        name="ffn",
    )(x3d, st, win, cw, cb, wout, g, b)


def _prep_layer_weights(l, P):
    W = GROUP_W
    b_in = P['b_in'][l]
    o_fox = 2 * W
    o_ff = o_fox + 3 * W
    o_sb = o_ff + N_HEADS
    o_dn = o_sb + 3 * W
    o_da = o_dn + 3 * W
    o_db = o_da + N_HEADS
    o_dg = o_db + N_HEADS
    pad = SMALL_W - 3 * N_HEADS

    def reorder(a):
        parts = [a[0:o_ff], a[o_sb:o_da], a[o_dg:o_dg + W], a[o_ff:o_ff + N_HEADS],
                 a[o_da:o_da + 2 * N_HEADS], jnp.zeros((pad,) + a.shape[1:], a.dtype)]
        return jnp.concatenate(parts, axis=0)

    if 'w_in_t' not in P:
        P['w_in_t'] = reorder(jnp.transpose(P['w_in'], (2, 0, 1))).astype(bf16)
        P['w_o_bf'] = P['w_o'].astype(bf16)
        P['ffn_win_bf'] = P['ffn_w_in'].astype(bf16)
        P['ffn_wout_bf'] = P['ffn_w_out'].astype(bf16)
    w_r = P['w_in_t'][:, l, :]
    b_r = reorder(b_in).reshape(1, -1)

    def block_diag(w4):
        eye = jnp.eye(N_HEADS, dtype=w4.dtype)
        return (w4[:, :, None, :] * eye[:, None, :, None]).reshape(W, W)

    small_row = lambda vals, off: jnp.zeros((1, SMALL_W), f32).at[0, off:off + N_HEADS].set(vals)
    r1 = lambda a: a.reshape(1, -1)
    return dict(
        w_in=w_r, b_in=b_r,
        lru_cw=P['lru_conv_w'][l], lru_cb=r1(P['lru_conv_b'][l]),
        lru_wa=block_diag(P['lru_w_a'][l]).astype(bf16), lru_wx=block_diag(P['lru_w_x'][l]).astype(bf16),
        lru_ba=r1(P['lru_b_a'][l]), lru_bx=r1(P['lru_b_x'][l]), lru_lam=r1(P['lru_lambda'][l]),
        gains=[r1(P['grp_norm_g'][l][i]) for i in range(3)],
        dn_cw=P['dn_conv_w'][l], dn_alog=small_row(P['dn_a_log'][l], N_HEADS),
        dn_dtb=small_row(P['dn_dt_bias'][l], N_HEADS), dn_ng=r1(jnp.tile(P['dn_norm_g'][l], N_HEADS)),
        w_o=P['w_o_bf'], ln1_g=r1(P['ln1_g'][l]), ln1_b=r1(P['ln1_b'][l]),
        ffn_win=P['ffn_win_bf'], ffn_cw=P['ffn_conv_w'][l], ffn_cb=r1(P['ffn_conv_b'][l]),
        ffn_wout=P['ffn_wout_bf'], ln2_g=r1(P['ln2_g'][l]), ln2_b=r1(P['ln2_b'][l]),
    )


def _embed_block_diag(S):
    B = S.shape[0]
    eye = jnp.eye(N_HEADS, dtype=S.dtype)
    return (S[:, :, :, None, :] * eye[None, :, None, :, None]).reshape(B, GROUP_W, GROUP_W)


def _extract_block_diag(Sbd):
    return jnp.stack([Sbd[:, h * HEAD_DIM:(h + 1) * HEAD_DIM, h * HEAD_DIM:(h + 1) * HEAD_DIM]
                      for h in range(N_HEADS)], axis=1)


def _trunk(x, st, LW, ln_in, alpha):
    B, L, D = x.shape
    W = GROUP_W
    F = LW[0]['ffn_wout'].shape[1]
    new = {n: [] for n in ('lru_conv', 'lru_h', 'fox_k', 'fox_v', 'fox_logf', 'sb_k', 'sb_v',
                           'dn_conv', 'dn_S', 'ffn_conv')}
    x2 = x.reshape(B * L, D)
    for l, w in enumerate(LW):
        outs = _proj(x2, ln_in[0], ln_in[1], w['w_in'], w['b_in'], do_ln=(l == 0))
        if l == 0:
            x2, outs = outs[0], outs[1:]
        lru, fq, fk, fv, sq, sk, sv, dqkv, dgate, small = [o.reshape(B, L, -1) for o in outs]
        if st is None:
            lru_conv = jnp.zeros((B, 3, W), f32)
            lru_h = jnp.zeros((B, 1, W), f32)
            dn_conv = jnp.zeros((B, 3, 3 * W), f32)
            dn_S = jnp.zeros((B, W, W), f32)
            ffn_conv = jnp.zeros((B, 2, F), f32)
            pfk = pfv = plf = psk = psv = None
        else:
            Pn = st['fox_k'].shape[2]
            lru_conv = st['lru_conv'][l]
            lru_h = st['lru_h'][l].reshape(B, 1, W)
            dn_conv = st['dn_conv'][l]
            dn_S = _embed_block_diag(st['dn_S'][l])
            ffn_conv = st['ffn_conv'][l]
            feat_major = lambda c: jnp.transpose(c.reshape(len(LW), B, Pn, W), (0, 1, 3, 2))
            pfk = feat_major(st['fox_k'])
            pfv = feat_major(st['fox_v'])
            plf = jnp.pad(st['fox_logf'][l], ((0, 0), (0, 0), (0, SMALL_W - N_HEADS)))
            psk = feat_major(st['sb_k'])
            psv = feat_major(st['sb_v'])

        ya, lru_conv_new, h_last = _lru(lru, lru_conv, lru_h, w['lru_cw'], w['lru_cb'], w['lru_wa'],
                                        w['lru_wx'], w['lru_ba'], w['lru_bx'], w['lru_lam'], w['gains'][0])
        kv_t = L >= ATT_SUB
        yb, logf, *fox_kv = _fox(fq, fk, fv, small, pfk, pfv, plf, w['gains'][1], kv_t, l)
        yc, *sb_kv = _sb(sq, sk, sv, psk, psv, w['gains'][2], kv_t, l)
        yd, dn_conv_new, S_new = _dn(dqkv, dgate, small, dn_conv, dn_S, w['dn_cw'], w['dn_alog'],
                                     w['dn_dtb'], w['dn_ng'])
        x1 = _merge(alpha, x2, ya.reshape(B * L, W), yb.reshape(B * L, W), yc.reshape(B * L, W),
                    yd.reshape(B * L, W), w['w_o'], w['ln1_g'], w['ln1_b'], l)
        x3, ffn_new = _ffn(alpha, x1.reshape(B, L, D), ffn_conv, w['ffn_win'], w['ffn_cw'], w['ffn_cb'],
                           w['ffn_wout'], w['ln2_g'], w['ln2_b'], l)
        x2 = x3.reshape(B * L, D)

        if kv_t:
            heads = lambda a: jnp.transpose(a.reshape(B, N_HEADS, HEAD_DIM, L), (0, 3, 1, 2))
            kv = fox_kv + sb_kv
        else:
            heads = lambda a: a.reshape(B, L, N_HEADS, HEAD_DIM)
            kv = [fk, fv, sk, sv]
        new['lru_conv'].append(lru_conv_new)
        new['lru_h'].append(h_last.reshape(B, W))
        new['fox_k'].append(heads(kv[0]))
        new['fox_v'].append(heads(kv[1]))
        new['fox_logf'].append(logf[:, :, :N_HEADS])
        new['sb_k'].append(heads(kv[2]))
        new['sb_v'].append(heads(kv[3]))
        new['dn_conv'].append(dn_conv_new)
        new['dn_S'].append(_extract_block_diag(S_new))
        new['ffn_conv'].append(ffn_new)
    return x2.reshape(B, L, D), {n: jnp.stack(v) for n, v in new.items()}


def kernel(x_prompt, x_sample, state_lru_conv, state_lru_h, cache_fox_k, cache_fox_v, cache_fox_logf,
           cache_sb_k, cache_sb_v, state_dn_conv, state_dn_S, state_ffn_conv, meta_tokens, ln_in_g, ln_in_b,
           w_in, b_in, lru_conv_w, lru_conv_b, lru_w_a, lru_b_a, lru_w_x, lru_b_x, lru_lambda, dn_conv_w,
           dn_a_log, dn_dt_bias, dn_norm_g, grp_norm_g, w_o, ln1_g, ln1_b, ffn_w_in, ffn_conv_w, ffn_conv_b,
           ffn_w_out, ln2_g, ln2_b):
    P = dict(w_in=w_in, b_in=b_in, lru_conv_w=lru_conv_w, lru_conv_b=lru_conv_b, lru_w_a=lru_w_a,
             lru_b_a=lru_b_a, lru_w_x=lru_w_x, lru_b_x=lru_b_x, lru_lambda=lru_lambda, dn_conv_w=dn_conv_w,
             dn_a_log=dn_a_log, dn_dt_bias=dn_dt_bias, dn_norm_g=dn_norm_g, grp_norm_g=grp_norm_g, w_o=w_o,
             ln1_g=ln1_g, ln1_b=ln1_b, ffn_w_in=ffn_w_in, ffn_conv_w=ffn_conv_w, ffn_conv_b=ffn_conv_b,
             ffn_w_out=ffn_w_out, ln2_g=ln2_g, ln2_b=ln2_b)
    depth = w_in.shape[0]
    assert x_prompt.shape[2] == D_MODEL
    alpha = (2.0 * depth) ** 0.25
    LW = [_prep_layer_weights(l, P) for l in range(depth)]
    ln_in = (ln_in_g.reshape(1, -1), ln_in_b.reshape(1, -1))

    bp = x_prompt.shape[0]
    meta = jnp.broadcast_to(meta_tokens.astype(x_prompt.dtype), (bp, N_META, D_MODEL))
    xp = jnp.concatenate([meta, x_prompt], axis=1)
    yp, ps = _trunk(xp, None, LW, ln_in, alpha)

    st_in = dict(lru_conv=state_lru_conv, lru_h=state_lru_h, fox_k=cache_fox_k, fox_v=cache_fox_v,
                 fox_logf=cache_fox_logf, sb_k=cache_sb_k, sb_v=cache_sb_v, dn_conv=state_dn_conv,
                 dn_S=state_dn_S, ffn_conv=state_ffn_conv)
    ys, ss = _trunk(x_sample, st_in, LW, ln_in, alpha)

    names = ('lru_conv', 'lru_h', 'fox_k', 'fox_v', 'fox_logf', 'sb_k', 'sb_v', 'dn_conv', 'dn_S', 'ffn_conv')
    return (yp[:, N_META:], ys) + tuple(ps[n] for n in names) + tuple(ss[n] for n in names)
```
